```python
import math
import jax, jax.numpy as jnp
from jax import lax
import numpy as np

D_MODEL = 2048
BATCH = 2
SEQ = 4096
DEPTH = 4

D_MIX = D_MODEL
D_POOL = D_MIX // 2
POOL_WINDOWS = (2, 4, 8, 16)
N_POOL_GROUPS = len(POOL_WINDOWS)
POOL_C = D_POOL // N_POOL_GROUPS
N_HEADS = 8
NOPE_DIM = 128
ROPE_DIM = 64
V_DIM = 128
D_ATT = N_HEADS * V_DIM
Q_LORA = D_MODEL // 4
KV_LORA = D_MODEL // 4
ROPE_THETA = 10000.0
Q_BLOCK = 128
D_IN = D_POOL + Q_LORA + KV_LORA + ROPE_DIM
D_FF = 5632
CONV_W = 3
PLE_DIM = 256
EPS = 1e-6

kernel_name = 'hybrid_pool_mla_convffn_ple'


def rms_norm(x, g):
    xf = x.astype(jnp.float32)
    y = xf * lax.rsqrt(jnp.mean(xf * xf, axis=-1, keepdims=True) + EPS)
    return (y * g.astype(jnp.float32)).astype(x.dtype)


def rope_tables(positions):
    inv_freq = 1.0 / (ROPE_THETA ** (jnp.arange(0, ROPE_DIM, 2, dtype=jnp.float32) / ROPE_DIM))
    ang = positions.astype(jnp.float32)[..., None] * inv_freq
    return jnp.cos(ang), jnp.sin(ang)


def apply_rope(x, cos, sin):
    half = x.shape[-1] // 2
    xf = x.astype(jnp.float32)
    x1, x2 = xf[..., :half], xf[..., half:]
    return jnp.concatenate([x1 * cos - x2 * sin, x2 * cos + x1 * sin], axis=-1).astype(x.dtype)


def multiscale_pool(u, pool_w, pool_scale):
    B, S, _ = u.shape
    uf = u.astype(jnp.float32)
    cs = jnp.cumsum(uf, axis=1)
    t = jnp.arange(S)
    outs = []
    for g, w in enumerate(POOL_WINDOWS):
        c = cs[..., g * POOL_C:(g + 1) * POOL_C]
        lag = jnp.pad(c, ((0, 0), (w, 0), (0, 0)))[:, :S]
        cnt = jnp.minimum(t + 1, w).astype(jnp.float32)[None, :, None]
        outs.append((c - lag) / cnt)
    pooled = jnp.stack(outs, axis=2)
    diff = (pooled - uf.reshape(B, S, N_POOL_GROUPS, POOL_C)).astype(u.dtype)
    y = jnp.einsum('bsgc,gcd->bsgd', diff, pool_w).reshape(B, S, D_POOL)
    return y * pool_scale


def latent_attention(c_q, c_kv, k_rope_raw, cos, sin, q_norm_g, w_uq, kv_norm_g, w_ukv):
    B, S, _ = c_q.shape
    q = (rms_norm(c_q, q_norm_g) @ w_uq).reshape(B, S, N_HEADS, NOPE_DIM + ROPE_DIM)
    q_nope = q[..., :NOPE_DIM]
    q_rope = apply_rope(q[..., NOPE_DIM:], cos[:, :, None], sin[:, :, None])
    kv = (rms_norm(c_kv, kv_norm_g) @ w_ukv).reshape(B, S, N_HEADS, NOPE_DIM + V_DIM)
    k_nope, v = kv[..., :NOPE_DIM], kv[..., NOPE_DIM:]
    k_rope = apply_rope(k_rope_raw, cos, sin)
    scale = 1.0 / math.sqrt(NOPE_DIM + ROPE_DIM)
    nb = S // Q_BLOCK
    qn = q_nope.reshape(B, nb, Q_BLOCK, N_HEADS, NOPE_DIM).transpose(1, 0, 2, 3, 4)
    qr = q_rope.reshape(B, nb, Q_BLOCK, N_HEADS, ROPE_DIM).transpose(1, 0, 2, 3, 4)
    k_pos = jnp.arange(S)

    def block(args):
        qn_b, qr_b, bi = args
        s = (jnp.einsum('bqhd,bkhd->bhqk', qn_b, k_nope).astype(jnp.float32)
             + jnp.einsum('bqhr,bkr->bhqk', qr_b, k_rope).astype(jnp.float32)) * scale
        q_pos = bi * Q_BLOCK + jnp.arange(Q_BLOCK)
        s = jnp.where(q_pos[:, None] >= k_pos[None, :], s, -jnp.inf)
        pr = jax.nn.softmax(s, axis=-1).astype(v.dtype)
        return jnp.einsum('bhqk,bkhd->bqhd', pr, v)

    out = lax.map(block, (qn, qr, jnp.arange(nb)))
    return out.transpose(1, 0, 2, 3, 4).reshape(B, S, D_ATT)


def causal_depthwise_conv(x, w, b):
    C = x.shape[-1]
    y = lax.conv_general_dilated(x, w[:, None, :].astype(x.dtype), window_strides=(1,),
                                 padding=[(CONV_W - 1, 0)],
                                 dimension_numbers=('NWC', 'WIO', 'NWC'),
                                 feature_group_count=C)
    return y + b


def conv_gated_ffn(x, w_up, conv_w, conv_b, w_down):
    gu = x @ w_up
    gate, up = gu[..., :D_FF], gu[..., D_FF:]
    gate = causal_depthwise_conv(gate, conv_w, conv_b)
    return (jax.nn.silu(gate) * up) @ w_down


def setup_inputs(seed: int = 0) -> dict:
    key = jax.random.key(seed)
    ks = jax.random.split(key, 24)
    f32 = jnp.float32

    def nrm(k, shape, fan_in):
        return jax.random.normal(k, shape, f32) * (fan_in ** -0.5)

    def gain(k, shape):
        return 1.0 + 0.02 * jax.random.normal(k, shape, f32)

    return {
        'x': jax.random.normal(ks[0], (BATCH, SEQ, D_MODEL), f32),
        'p': jax.random.normal(ks[1], (DEPTH, BATCH, SEQ, PLE_DIM), f32),
        'positions': jnp.broadcast_to(jnp.arange(SEQ, dtype=jnp.int32), (BATCH, SEQ)),
        'norm_mix_g': gain(ks[2], (DEPTH, D_MODEL)),
        'w_in': nrm(ks[3], (DEPTH, D_MODEL, D_IN), D_MODEL),
        'pool_w': nrm(ks[4], (DEPTH, N_POOL_GROUPS, POOL_C, POOL_C), POOL_C),
        'pool_scale': gain(ks[5], (DEPTH, D_POOL)),
        'q_norm_g': gain(ks[6], (DEPTH, Q_LORA)),
        'w_uq': nrm(ks[7], (DEPTH, Q_LORA, N_HEADS * (NOPE_DIM + ROPE_DIM)), Q_LORA),
        'kv_norm_g': gain(ks[8], (DEPTH, KV_LORA)),
        'w_ukv': nrm(ks[9], (DEPTH, KV_LORA, N_HEADS * (NOPE_DIM + V_DIM)), KV_LORA),
        'w_out': nrm(ks[10], (DEPTH, D_MIX, D_MODEL), D_MIX),
        'norm_ffn_g': gain(ks[11], (DEPTH, D_MODEL)),
        'w_up': nrm(ks[12], (DEPTH, D_MODEL, 2 * D_FF), D_MODEL),
        'conv_w': nrm(ks[13], (DEPTH, CONV_W, D_FF), CONV_W),
        'conv_b': 0.01 * jax.random.normal(ks[14], (DEPTH, D_FF), f32),
        'w_down': nrm(ks[15], (DEPTH, D_FF, D_MODEL), D_FF),
        'norm_ple_g': gain(ks[16], (DEPTH, D_MODEL)),
        'w_ple': nrm(ks[17], (DEPTH, PLE_DIM, D_MODEL), PLE_DIM),
        'w_ple_gate': nrm(ks[18], (DEPTH, D_MODEL, D_MODEL), D_MODEL),
        'final_norm_g': gain(ks[19], (D_MODEL,)),
    }


def reference(x, p, positions, norm_mix_g, w_in, pool_w, pool_scale, q_norm_g, w_uq,
              kv_norm_g, w_ukv, w_out, norm_ffn_g, w_up, conv_w, conv_b, w_down,
              norm_ple_g, w_ple, w_ple_gate, final_norm_g):
    cos, sin = rope_tables(positions)
    o1 = D_POOL
    o2 = o1 + Q_LORA
    o3 = o2 + KV_LORA
    h = x
    for i in range(DEPTH):
        u = rms_norm(h, norm_mix_g[i]) @ w_in[i]
        y_pool = multiscale_pool(u[..., :o1], pool_w[i], pool_scale[i])
        y_att = latent_attention(u[..., o1:o2], u[..., o2:o3], u[..., o3:], cos, sin,
                                 q_norm_g[i], w_uq[i], kv_norm_g[i], w_ukv[i])
        h = h + jnp.concatenate([y_pool, y_att], axis=-1) @ w_out[i]
        h = h + conv_gated_ffn(rms_norm(h, norm_ffn_g[i]), w_up[i], conv_w[i], conv_b[i], w_down[i])
        gate = jax.nn.sigmoid(rms_norm(h, norm_ple_g[i]) @ w_ple_gate[i])
        h = h + (p[i] @ w_ple[i]) * gate
    return rms_norm(h, final_norm_g)
```

```python
import functools
import math

import jax
import jax.numpy as jnp
from jax import lax
from jax.experimental import pallas as pl
from jax.experimental.pallas import tpu as pltpu

F32 = jnp.float32
BF16 = jnp.bfloat16

POOL_WINDOWS = (2, 4, 8, 16)
N_POOL_GROUPS = len(POOL_WINDOWS)
N_HEADS = 8
NOPE_DIM = 128
ROPE_DIM = 64
V_DIM = 128
ROPE_THETA = 10000.0
CONV_W = 3
EPS = 1e-6

LANES = 128
SUBLANES = 8
V7X_SCOPED_VMEM_BYTES = 60000 * 1024

POOL_HALO = 16
CONV_HALO = SUBLANES

FRONT_BM = 512
ATTN_BQ = 512
OUT_BM = 512
FFN_UP_BM = 1024
FFN_UP_BN = 512
FFN_DOWN_BM = 256
PLE_BM = 512


def _rms(x, g):
    ms = jnp.mean(x * x, axis=-1, keepdims=True)
    return (x * lax.rsqrt(ms + EPS)) * g


def _sigmoid(x):
    return 1.0 / (1.0 + jnp.exp(-x))


def _resident(shape):
    nd = len(shape)
    return pl.BlockSpec(shape, lambda *_: (0,) * nd, pipeline_mode=pl.Buffered(1))


def _params(n_axes, vmem_bytes):
    return pltpu.CompilerParams(
        dimension_semantics=("arbitrary",) * n_axes,
        vmem_limit_bytes=min(int(vmem_bytes), V7X_SCOPED_VMEM_BYTES),
    )


def _front_kernel(h_ref, g_ref, win_ref, cos_ref, sin_ref, pw_ref, ps_ref,
                  gq_ref, wq_ref, gkv_ref, wkv_ref,
                  ypool_ref, qn_ref, qr_ref, kn_ref, v_ref, kr_ref,
                  pbuf_ref, *, bm, tiles_per_seq, d_pool, q_lora, kv_lora):
    i = pl.program_id(0)
    seq_tile = i % tiles_per_seq
    pool_c = d_pool // N_POOL_GROUPS
    o1 = d_pool
    o2 = o1 + q_lora
    o3 = o2 + kv_lora

    xn = _rms(h_ref[...], g_ref[...]).astype(BF16)
    u = jnp.dot(xn, win_ref[...], preferred_element_type=F32)

    cos = cos_ref[...]
    sin = sin_ref[...]
    kr_ref[...] = (u[:, o3:o3 + LANES] * cos + u[:, o3 + LANES:o3 + 2 * LANES] * sin).astype(BF16)

    cqn = _rms(u[:, o1:o2], gq_ref[...]).astype(BF16)
    q = jnp.dot(cqn, wq_ref[...], preferred_element_type=F32)
    hn = N_HEADS * NOPE_DIM
    hr = N_HEADS * LANES
    qn_ref[...] = q[:, :hn].astype(BF16)
    cos_h = jnp.concatenate([cos] * N_HEADS, axis=1)
    sin_h = jnp.concatenate([sin] * N_HEADS, axis=1)
    qr_ref[...] = (q[:, hn:hn + hr] * cos_h + q[:, hn + hr:hn + 2 * hr] * sin_h).astype(BF16)

    ckvn = _rms(u[:, o2:o3], gkv_ref[...]).astype(BF16)
    kv = jnp.dot(ckvn, wkv_ref[...], preferred_element_type=F32)
    kn_ref[...] = kv[:, :hn].astype(BF16)
    v_ref[...] = kv[:, hn:].astype(BF16)

    @pl.when(seq_tile == 0)
    def _():
        pbuf_ref[0:POOL_HALO, :] = jnp.zeros((POOL_HALO, d_pool), F32)

    pbuf_ref[POOL_HALO:POOL_HALO + bm, :] = u[:, :o1]
    t1 = lax.broadcasted_iota(jnp.int32, (bm, LANES), 0) + (seq_tile * bm + 1)
    for grp, w in enumerate(POOL_WINDOWS):
        c0, c1 = grp * pool_c, (grp + 1) * pool_c
        ext = pbuf_ref[:, c0:c1]
        acc = ext + pltpu.roll(ext, 1, 0)
        span = 2
        while span < w:
            acc = acc + pltpu.roll(acc, span, 0)
            span *= 2
        win = acc[POOL_HALO:, :]
        inv = 1.0 / jnp.minimum(t1, w).astype(F32)
        inv = jnp.concatenate([inv] * (pool_c // LANES), axis=1)
        diff = (win * inv - ext[POOL_HALO:, :]).astype(BF16)
        y = jnp.dot(diff, pw_ref[grp], preferred_element_type=F32) * ps_ref[:, c0:c1]
        ypool_ref[:, c0:c1] = y.astype(BF16)
    pbuf_ref[0:POOL_HALO, :] = pbuf_ref[bm:bm + POOL_HALO, :]


def _front(h, g, win, cos128, sin128, pw, ps, gq, wq, gkv, wkv, *, seq, d_pool, q_lora, kv_lora):
    t, d = h.shape
    bm = FRONT_BM
    hn = N_HEADS * NOPE_DIM
    row = lambda n: pl.BlockSpec((bm, n), lambda i: (i, 0))
    kern = functools.partial(_front_kernel, bm=bm, tiles_per_seq=seq // bm,
                             d_pool=d_pool, q_lora=q_lora, kv_lora=kv_lora)
    weights = win.size * 2 + wq.size * 2 + wkv.size * 2 + pw.size * 2
    tiles = 2 * bm * (d * 4 + 2 * LANES * 4 + (d_pool + 4 * hn + LANES) * 2)
    temps = bm * (win.shape[1] + wq.shape[1] + wkv.shape[1] + 4 * d_pool) * 4 + bm * d * 6
    return pl.pallas_call(
        kern,
        grid=(t // bm,),
        in_specs=[row(d), _resident(g.shape), _resident(win.shape), row(LANES), row(LANES),
                  _resident(pw.shape), _resident(ps.shape), _resident(gq.shape),
                  _resident(wq.shape), _resident(gkv.shape), _resident(wkv.shape)],
        out_specs=[row(d_pool), row(hn), row(hn), row(hn), row(hn), row(LANES)],
        out_shape=[jax.ShapeDtypeStruct((t, d_pool), BF16)] + [jax.ShapeDtypeStruct((t, hn), BF16)] * 4
                  + [jax.ShapeDtypeStruct((t, LANES), BF16)],
        scratch_shapes=[pltpu.VMEM((POOL_HALO + bm, d_pool), F32)],
        compiler_params=_params(1, weights + tiles + temps + (POOL_HALO + bm) * d_pool * 4),
        name="front",
    )(h, g, win, cos128, sin128, pw, ps, gq, wq, gkv, wkv)


def _attn_kernel(qn_ref, qr_ref, kn_ref, kr_ref, v_ref, o_ref, *, bq, scale):
    qi = pl.program_id(2)
    q = jnp.concatenate([qn_ref[...], qr_ref[...]], axis=-1)

    def step(j, carry, masked):
        m, l, acc = carry
        start = pl.multiple_of(j * bq, bq)
        k = jnp.concatenate([kn_ref[pl.ds(start, bq), :], kr_ref[pl.ds(start, bq), :]], axis=-1)
        s = lax.dot_general(q, k, (((1,), (1,)), ((), ())), preferred_element_type=F32) * scale
        if masked:
            r = lax.broadcasted_iota(jnp.int32, (bq, bq), 0)
            c = lax.broadcasted_iota(jnp.int32, (bq, bq), 1)
            s = jnp.where(r >= c, s, -jnp.inf)
        m_new = jnp.maximum(m, jnp.max(s, axis=-1, keepdims=True))
        alpha = jnp.exp(m - m_new)
        p = jnp.exp(s - m_new)
        l = alpha * l + jnp.sum(p, axis=-1, keepdims=True)
        acc = alpha * acc + jnp.dot(p.astype(BF16), v_ref[pl.ds(start, bq), :],
                                    preferred_element_type=F32)
        return m_new, l, acc

    init = (jnp.full((bq, 1), -jnp.inf, F32), jnp.zeros((bq, 1), F32), jnp.zeros((bq, V_DIM), F32))
    carry = lax.fori_loop(0, qi, lambda j, c: step(j, c, False), init)
    _, l, acc = step(qi, carry, True)
    o_ref[...] = (acc / l).astype(BF16)


def _attention(qn, qr, kn, kr, v, *, batch, seq):
    bq = ATTN_BQ
    hn = N_HEADS * NOPE_DIM
    r3 = lambda a: a.reshape(batch, seq, a.shape[-1])
    qspec = pl.BlockSpec((None, bq, LANES), lambda b, h, qi: (b, qi, h))
    kspec = pl.BlockSpec((None, seq, LANES), lambda b, h, qi: (b, 0, h))
    krspec = pl.BlockSpec((None, seq, LANES), lambda b, h, qi: (b, 0, 0))
    kern = functools.partial(_attn_kernel, bq=bq, scale=1.0 / math.sqrt(NOPE_DIM + ROPE_DIM))
    vmem = 2 * 3 * seq * LANES * 2 + 2 * 3 * bq * LANES * 2 + 6 * bq * bq * 4 + (8 << 20)
    out = pl.pallas_call(
        kern,
        grid=(batch, N_HEADS, seq // bq),
        in_specs=[qspec, qspec, kspec, krspec, kspec],
        out_specs=qspec,
        out_shape=jax.ShapeDtypeStruct((batch, seq, hn), BF16),
        compiler_params=_params(3, vmem),
        name="attention",
    )(r3(qn), r3(qr), r3(kn), r3(kr), r3(v))
    return out.reshape(batch * seq, hn)


def _out_proj_kernel(yp_ref, ya_ref, w_ref, h_ref, o_ref):
    y = jnp.concatenate([yp_ref[...], ya_ref[...]], axis=-1)
    o_ref[...] = h_ref[...] + jnp.dot(y, w_ref[...], preferred_element_type=F32)


def _out_proj(yp, ya, w, h):
    t, d = h.shape
    bm = OUT_BM
    row = lambda n: pl.BlockSpec((bm, n), lambda i: (i, 0))
    vmem = w.size * 2 + 2 * bm * (yp.shape[1] + ya.shape[1]) * 2 + 4 * bm * d * 4 + 2 * bm * d * 4
    return pl.pallas_call(
        _out_proj_kernel,
        grid=(t // bm,),
        in_specs=[row(yp.shape[1]), row(ya.shape[1]), _resident(w.shape), row(d)],
        out_specs=row(d),
        out_shape=jax.ShapeDtypeStruct((t, d), F32),
        compiler_params=_params(1, vmem + (4 << 20)),
        name="out_proj",
    )(yp, ya, w, h)


def _ffn_up_kernel(h_ref, g_ref, wg_ref, wu_ref, cw_ref, cb_ref, o_ref,
                   xn_ref, gbuf_ref, carry_ref, *, bm, tiles_per_seq):
    i = pl.program_id(0)
    j = pl.program_id(1)
    bn = o_ref.shape[1]

    @pl.when(j == 0)
    def _():
        xn_ref[...] = _rms(h_ref[...], g_ref[...]).astype(BF16)

    xn = xn_ref[...]
    gate = jnp.dot(xn, wg_ref[...], preferred_element_type=F32)
    up = jnp.dot(xn, wu_ref[...], preferred_element_type=F32)

    @pl.when(i % tiles_per_seq == 0)
    def _():
        carry_ref[j] = jnp.zeros((CONV_HALO, bn), F32)

    gbuf_ref[0:CONV_HALO, :] = carry_ref[j]
    gbuf_ref[CONV_HALO:CONV_HALO + bm, :] = gate
    carry_ref[j] = gate[bm - CONV_HALO:, :]
    g1 = gbuf_ref[CONV_HALO - 1:CONV_HALO - 1 + bm, :]
    g2 = gbuf_ref[CONV_HALO - 2:CONV_HALO - 2 + bm, :]
    cw = cw_ref[...]
    c = cw[0:1, :] * g2 + cw[1:2, :] * g1 + cw[2:3, :] * gate + cb_ref[...]
    o_ref[...] = (c * _sigmoid(c) * up).astype(BF16)


def _ffn_up(h, g, w_up, conv_w, conv_b, *, seq):
    t, d = h.shape
    d_ff = conv_w.shape[1]
    bm, bn = FFN_UP_BM, FFN_UP_BN
    nj = d_ff // bn
    kern = functools.partial(_ffn_up_kernel, bm=bm, tiles_per_seq=seq // bm)
    vmem = (2 * bm * d * 4 + bm * d * 2 + 2 * 2 * d * bn * 2 + 2 * bm * bn * 2
            + (CONV_HALO + bm) * bn * 4 + nj * CONV_HALO * bn * 4 + 6 * bm * bn * 4 + bm * d * 4)
    return pl.pallas_call(
        kern,
        grid=(t // bm, nj),
        in_specs=[pl.BlockSpec((bm, d), lambda i, j: (i, 0)),
                  _resident(g.shape),
                  pl.BlockSpec((d, bn), lambda i, j: (0, j)),
                  pl.BlockSpec((d, bn), lambda i, j: (0, j + nj)),
                  pl.BlockSpec((CONV_W, bn), lambda i, j: (0, j)),
                  pl.BlockSpec((1, bn), lambda i, j: (0, j))],
        out_specs=pl.BlockSpec((bm, bn), lambda i, j: (i, j)),
        out_shape=jax.ShapeDtypeStruct((t, d_ff), BF16),
        scratch_shapes=[pltpu.VMEM((bm, d), BF16),
                        pltpu.VMEM((CONV_HALO + bm, bn), F32),
                        pltpu.VMEM((nj, CONV_HALO, bn), F32)],
        compiler_params=_params(2, vmem + (4 << 20)),
        name="ffn_up",
    )(h, g, w_up, w_up, conv_w, conv_b)


def _ffn_down_kernel(a_ref, w_ref, h_ref, o_ref):
    o_ref[...] = h_ref[...] + jnp.dot(a_ref[...], w_ref[...], preferred_element_type=F32)


def _ffn_down(act, w, h):
    t, d = h.shape
    d_ff = act.shape[1]
    bm = FFN_DOWN_BM
    vmem = w.size * 2 + 2 * bm * d_ff * 2 + 4 * bm * d * 4 + 2 * bm * d * 4
    return pl.pallas_call(
        _ffn_down_kernel,
        grid=(t // bm,),
        in_specs=[pl.BlockSpec((bm, d_ff), lambda i: (i, 0)), _resident(w.shape),
                  pl.BlockSpec((bm, d), lambda i: (i, 0))],
        out_specs=pl.BlockSpec((bm, d), lambda i: (i, 0)),
        out_shape=jax.ShapeDtypeStruct((t, d), F32),
        compiler_params=_params(1, vmem + (4 << 20)),
        name="ffn_down",
    )(act, w, h)


def _ple_kernel(h_ref, g_ref, wgate_ref, p_ref, wple_ref, gfin_ref, o_ref, *, final):
    h = h_ref[...]
    xn = _rms(h, g_ref[...]).astype(BF16)
    gate = _sigmoid(jnp.dot(xn, wgate_ref[...], preferred_element_type=F32))
    emb = jnp.dot(p_ref[...].astype(BF16), wple_ref[...], preferred_element_type=F32)
    out = h + emb * gate
    if final:
        out = _rms(out, gfin_ref[...])
    o_ref[...] = out


def _ple(h, g, w_gate, p, w_ple, g_final, *, final):
    t, d = h.shape
    bm = PLE_BM
    row = lambda n: pl.BlockSpec((bm, n), lambda i: (i, 0))
    vmem = (w_gate.size + w_ple.size) * 2 + 4 * bm * d * 4 + 2 * bm * p.shape[1] * 4 + 5 * bm * d * 4
    return pl.pallas_call(
        functools.partial(_ple_kernel, final=final),
        grid=(t // bm,),
        in_specs=[row(d), _resident(g.shape), _resident(w_gate.shape), row(p.shape[1]),
                  _resident(w_ple.shape), _resident(g_final.shape)],
        out_specs=row(d),
        out_shape=jax.ShapeDtypeStruct((t, d), F32),
        compiler_params=_params(1, vmem + (4 << 20)),
        name="ple",
    )(h, g, w_gate, p, w_ple, g_final)


def _rope_cols(w):
    half = ROPE_DIM // 2
    x1, x2 = w[..., :half], w[..., half:]
    pad = jnp.zeros(w.shape[:-1] + (LANES - ROPE_DIM,), w.dtype)
    return jnp.concatenate([x1, x2, pad], axis=-1), jnp.concatenate([-x2, x1, pad], axis=-1)


def _prep_w_in(w_in, o3):
    plain, rot = _rope_cols(w_in[:, o3:])
    return jnp.concatenate([w_in[:, :o3], plain, rot], axis=1).astype(BF16)


def _prep_w_uq(w_uq):
    k = w_uq.shape[0]
    w = w_uq.reshape(k, N_HEADS, NOPE_DIM + ROPE_DIM)
    plain, rot = _rope_cols(w[:, :, NOPE_DIM:])
    return jnp.concatenate([w[:, :, :NOPE_DIM].reshape(k, -1), plain.reshape(k, -1),
                            rot.reshape(k, -1)], axis=1).astype(BF16)


def _prep_w_ukv(w_ukv):
    k = w_ukv.shape[0]
    w = w_ukv.reshape(k, N_HEADS, NOPE_DIM + V_DIM)
    return jnp.concatenate([w[:, :, :NOPE_DIM].reshape(k, -1), w[:, :, NOPE_DIM:].reshape(k, -1)],
                           axis=1).astype(BF16)


def kernel(x, p, positions, norm_mix_g, w_in, pool_w, pool_scale, q_norm_g, w_uq, kv_norm_g, w_ukv,
           w_out, norm_ffn_g, w_up, conv_w, conv_b, w_down, norm_ple_g, w_ple, w_ple_gate, final_norm_g):
    batch, seq, d = x.shape
    depth = w_in.shape[0]
    t = batch * seq
    d_pool = pool_scale.shape[1]
    q_lora = q_norm_g.shape[1]
    kv_lora = kv_norm_g.shape[1]
    o3 = d_pool + q_lora + kv_lora

    inv_freq = 1.0 / (ROPE_THETA ** (jnp.arange(0, ROPE_DIM, 2, dtype=F32) / ROPE_DIM))
    ang = positions.astype(F32)[..., None] * inv_freq
    zpad = jnp.zeros((batch, seq, LANES - ROPE_DIM), F32)
    cos128 = jnp.concatenate([jnp.cos(ang), jnp.cos(ang), zpad], axis=-1).reshape(t, LANES)
    sin128 = jnp.concatenate([jnp.sin(ang), jnp.sin(ang), zpad], axis=-1).reshape(t, LANES)

    row = lambda a: a.reshape(1, -1)
    h = x.reshape(t, d)
    for i in range(depth):
        ypool, qn, qr, kn, v, kr = _front(
            h, row(norm_mix_g[i]), _prep_w_in(w_in[i], o3), cos128, sin128,
            pool_w[i].astype(BF16), row(pool_scale[i]), row(q_norm_g[i]), _prep_w_uq(w_uq[i]),
            row(kv_norm_g[i]), _prep_w_ukv(w_ukv[i]),
            seq=seq, d_pool=d_pool, q_lora=q_lora, kv_lora=kv_lora)
        yatt = _attention(qn, qr, kn, kr, v, batch=batch, seq=seq)
        h = _out_proj(ypool, yatt, w_out[i].astype(BF16), h)
        act = _ffn_up(h, row(norm_ffn_g[i]), w_up[i].astype(BF16), conv_w[i], row(conv_b[i]), seq=seq)
        h = _ffn_down(act, w_down[i].astype(BF16), h)
        h = _ple(h, row(norm_ple_g[i]), w_ple_gate[i].astype(BF16), p[i].reshape(t, -1),
                 w_ple[i].astype(BF16), row(final_norm_g), final=(i == depth - 1))
    return h.reshape(batch, seq, d)
```

```python
import functools
import math

import jax
import jax.numpy as jnp
from jax import lax
from jax.experimental import pallas as pl
from jax.experimental.pallas import tpu as pltpu

F32 = jnp.float32
BF16 = jnp.bfloat16

POOL_WINDOWS = (2, 4, 8, 16)
N_POOL_GROUPS = len(POOL_WINDOWS)
N_HEADS = 8
NOPE_DIM = 128
ROPE_DIM = 64
V_DIM = 128
ROPE_THETA = 10000.0
CONV_W = 3
EPS = 1e-6

LANES = 128
SUBLANES = 8
V7X_SCOPED_VMEM_BYTES = 60000 * 1024

POOL_HALO = 16

FRONT_BM = 512
ATTN_BLK = 512
ATTN_HEADS = 2
OUT_BM = 512
FFN_UP_BM = 1024
FFN_UP_BN = 512
FFN_DOWN_BM = 256
PLE_BM = 512
ROW_SUB = 256


def _rms(x, g):
    ms = jnp.mean(x * x, axis=-1, keepdims=True)
    return (x * lax.rsqrt(ms + EPS)) * g


def _sigmoid(x):
    return 1.0 / (1.0 + jnp.exp(-x))


def _resident(shape):
    nd = len(shape)
    return pl.BlockSpec(shape, lambda *_: (0,) * nd, pipeline_mode=pl.Buffered(1))


def _params(n_axes, vmem_bytes):
    return pltpu.CompilerParams(
        dimension_semantics=("arbitrary",) * n_axes,
        vmem_limit_bytes=min(int(vmem_bytes), V7X_SCOPED_VMEM_BYTES),
    )


def _front_kernel(h_ref, g_ref, win_ref, cos_ref, sin_ref, pw_ref, ps_ref,
                  gq_ref, wq_ref, gkv_ref, wkv_ref,
                  ypool_ref, qn_ref, qr_ref, kn_ref, v_ref, kr_ref,
                  pbuf_ref, *, bm, tiles_per_seq, d_pool, q_lora, kv_lora):
    i = pl.program_id(0)
    seq_tile = i % tiles_per_seq
    pool_c = d_pool // N_POOL_GROUPS
    o1 = d_pool
    o2 = o1 + q_lora
    o3 = o2 + kv_lora

    xn = _rms(h_ref[...], g_ref[...]).astype(BF16)
    u = jnp.dot(xn, win_ref[...], preferred_element_type=F32)

    cos = cos_ref[...]
    sin = sin_ref[...]
    kr_ref[...] = (u[:, o3:o3 + LANES] * cos + u[:, o3 + LANES:o3 + 2 * LANES] * sin).astype(BF16)

    cqn = _rms(u[:, o1:o2], gq_ref[...]).astype(BF16)
    q = jnp.dot(cqn, wq_ref[...], preferred_element_type=F32)
    hn = N_HEADS * NOPE_DIM
    hr = N_HEADS * LANES
    qn_ref[...] = q[:, :hn].astype(BF16)
    cos_h = jnp.concatenate([cos] * N_HEADS, axis=1)
    sin_h = jnp.concatenate([sin] * N_HEADS, axis=1)
    qr_ref[...] = (q[:, hn:hn + hr] * cos_h + q[:, hn + hr:hn + 2 * hr] * sin_h).astype(BF16)

    ckvn = _rms(u[:, o2:o3], gkv_ref[...]).astype(BF16)
    kv = jnp.dot(ckvn, wkv_ref[...], preferred_element_type=F32)
    kn_ref[...] = kv[:, :hn].astype(BF16)
    v_ref[...] = kv[:, hn:].astype(BF16)

    @pl.when(seq_tile == 0)
    def _():
        pbuf_ref[0:POOL_HALO, :] = jnp.zeros((POOL_HALO, d_pool), F32)

    pbuf_ref[POOL_HALO:POOL_HALO + bm, :] = u[:, :o1]
    t1 = lax.broadcasted_iota(jnp.int32, (bm, LANES), 0) + (seq_tile * bm + 1)
    for grp, w in enumerate(POOL_WINDOWS):
        c0, c1 = grp * pool_c, (grp + 1) * pool_c
        ext = pbuf_ref[:, c0:c1]
        acc = ext + pltpu.roll(ext, 1, 0)
        span = 2
        while span < w:
            acc = acc + pltpu.roll(acc, span, 0)
            span *= 2
        win = acc[POOL_HALO:, :]
        inv = 1.0 / jnp.minimum(t1, w).astype(F32)
        inv = jnp.concatenate([inv] * (pool_c // LANES), axis=1)
        diff = (win * inv - ext[POOL_HALO:, :]).astype(BF16)
        y = jnp.dot(diff, pw_ref[grp], preferred_element_type=F32) * ps_ref[:, c0:c1]
        ypool_ref[:, c0:c1] = y.astype(BF16)
    pbuf_ref[0:POOL_HALO, :] = pbuf_ref[bm:bm + POOL_HALO, :]


def _front(h, g, win, cos128, sin128, pw, ps, gq, wq, gkv, wkv, *, seq, d_pool, q_lora, kv_lora):
    t, d = h.shape
    bm = FRONT_BM
    hn = N_HEADS * NOPE_DIM
    row = lambda n: pl.BlockSpec((bm, n), lambda i: (i, 0))
    kern = functools.partial(_front_kernel, bm=bm, tiles_per_seq=seq // bm,
                             d_pool=d_pool, q_lora=q_lora, kv_lora=kv_lora)
    weights = win.size * 2 + wq.size * 2 + wkv.size * 2 + pw.size * 2
    tiles = 2 * bm * (d * 4 + 2 * LANES * 4 + (d_pool + 4 * hn + LANES) * 2)
    temps = bm * (win.shape[1] + wq.shape[1] + wkv.shape[1] + 4 * d_pool) * 4 + bm * d * 6
    return pl.pallas_call(
        kern,
        grid=(t // bm,),
        in_specs=[row(d), _resident(g.shape), _resident(win.shape), row(LANES), row(LANES),
                  _resident(pw.shape), _resident(ps.shape), _resident(gq.shape),
                  _resident(wq.shape), _resident(gkv.shape), _resident(wkv.shape)],
        out_specs=[row(d_pool), row(hn), row(hn), row(hn), row(hn), row(LANES)],
        out_shape=[jax.ShapeDtypeStruct((t, d_pool), BF16)] + [jax.ShapeDtypeStruct((t, hn), BF16)] * 4
                  + [jax.ShapeDtypeStruct((t, LANES), BF16)],
        scratch_shapes=[pltpu.VMEM((POOL_HALO + bm, d_pool), F32)],
        compiler_params=_params(1, weights + tiles + temps + (POOL_HALO + bm) * d_pool * 4),
        name="front",
    )(h, g, win, cos128, sin128, pw, ps, gq, wq, gkv, wkv)


def _attn_kernel(qn_ref, qr_ref, kn_ref, kr_ref, v_ref, o_ref, vt_ref, *, seq, blk, c2, heads):
    nblk = seq // blk
    nt_dims = (((1,), (1,)), ((), ()))

    def rows(c):
        return pl.ds(pl.multiple_of(c * blk, blk), blk)

    def load_vt(c, _):
        for hh in range(heads):
            v = v_ref[rows(c), hh * V_DIM:(hh + 1) * V_DIM]
            vt_ref[hh, c] = v.astype(F32).T.astype(BF16)
        return 0

    lax.fori_loop(0, nblk, load_vt, 0)

    def q_body(qi, _):
        q = [jnp.concatenate([qn_ref[rows(qi), hh * NOPE_DIM:(hh + 1) * NOPE_DIM],
                              qr_ref[rows(qi), hh * LANES:(hh + 1) * LANES]], axis=-1)
             for hh in range(heads)]

        def kv_step(kj, carry, masked):
            kr = kr_ref[rows(kj), :]
            out = []
            for hh in range(heads):
                m, l, acc = carry[hh]
                k = jnp.concatenate([kn_ref[rows(kj), hh * NOPE_DIM:(hh + 1) * NOPE_DIM], kr], axis=-1)
                t = lax.dot_general(k, q[hh], nt_dims, preferred_element_type=F32) * c2
                if masked:
                    key = lax.broadcasted_iota(jnp.int32, (blk, blk), 0)
                    qry = lax.broadcasted_iota(jnp.int32, (blk, blk), 1)
                    t = jnp.where(key <= qry, t, -jnp.inf)
                m_new = jnp.maximum(m, jnp.max(t, axis=0, keepdims=True))
                alpha = jnp.exp2(m - m_new)
                p = jnp.exp2(t - m_new)
                l = alpha * l + jnp.sum(p, axis=0, keepdims=True)
                acc = alpha * acc + jnp.dot(vt_ref[hh, kj], p.astype(BF16),
                                            preferred_element_type=F32)
                out.append((m_new, l, acc))
            return tuple(out)

        init = tuple((jnp.full((1, blk), -jnp.inf, F32), jnp.zeros((1, blk), F32),
                      jnp.zeros((V_DIM, blk), F32)) for _ in range(heads))
        carry = lax.fori_loop(0, qi, lambda kj, c: kv_step(kj, c, False), init)
        carry = kv_step(qi, carry, True)
        for hh in range(heads):
            _, l, acc = carry[hh]
            o_ref[rows(qi), hh * V_DIM:(hh + 1) * V_DIM] = (acc / l).T.astype(BF16)
        return 0

    lax.fori_loop(0, nblk, q_body, 0)


def _attention(qn, qr, kn, kr, v, *, batch, seq):
    blk, heads = ATTN_BLK, ATTN_HEADS
    hn = N_HEADS * NOPE_DIM
    w = heads * LANES
    r3 = lambda a: a.reshape(batch, seq, a.shape[-1])
    hspec = pl.BlockSpec((None, seq, w), lambda b, hp: (b, 0, hp))
    krspec = pl.BlockSpec((None, seq, LANES), lambda b, hp: (b, 0, 0))
    c2 = math.log2(math.e) / math.sqrt(NOPE_DIM + ROPE_DIM)
    kern = functools.partial(_attn_kernel, seq=seq, blk=blk, c2=c2, heads=heads)
    vmem = 2 * (5 * seq * w + seq * LANES) * 2 + seq * w * 2 + heads * 8 * blk * blk * 4 + (8 << 20)
    out = pl.pallas_call(
        kern,
        grid=(batch, N_HEADS // heads),
        in_specs=[hspec, hspec, hspec, krspec, hspec],
        out_specs=hspec,
        out_shape=jax.ShapeDtypeStruct((batch, seq, hn), BF16),
        scratch_shapes=[pltpu.VMEM((heads, seq // blk, V_DIM, blk), BF16)],
        compiler_params=_params(2, vmem),
        name="attention",
    )(r3(qn), r3(qr), r3(kn), r3(kr), r3(v))
    return out.reshape(batch * seq, hn)


def _out_proj_kernel(yp_ref, ya_ref, w_ref, h_ref, g_ref, o_ref, xn_ref, *, bm):
    for r0 in range(0, bm, ROW_SUB):
        rs = slice(r0, r0 + ROW_SUB)
        y = jnp.concatenate([yp_ref[rs, :], ya_ref[rs, :]], axis=-1)
        hn = h_ref[rs, :] + jnp.dot(y, w_ref[...], preferred_element_type=F32)
        o_ref[rs, :] = hn
        xn_ref[rs, :] = _rms(hn, g_ref[...]).astype(BF16)


def _out_proj(yp, ya, w, h, g):
    t, d = h.shape
    bm = OUT_BM
    row = lambda n: pl.BlockSpec((bm, n), lambda i: (i, 0))
    vmem = w.size * 2 + 2 * bm * (yp.shape[1] + ya.shape[1]) * 2 + 4 * bm * d * 4 + 2 * bm * d * 2
    return pl.pallas_call(
        functools.partial(_out_proj_kernel, bm=bm),
        grid=(t // bm,),
        in_specs=[row(yp.shape[1]), row(ya.shape[1]), _resident(w.shape), row(d), _resident(g.shape)],
        out_specs=[row(d), row(d)],
        out_shape=[jax.ShapeDtypeStruct((t, d), F32), jax.ShapeDtypeStruct((t, d), BF16)],
        compiler_params=_params(1, vmem + 4 * bm * d * 4 + (4 << 20)),
        name="out_proj",
    )(yp, ya, w, h, g)


def _shift_rows(cur, prev_tail, k):
    rolled = pltpu.roll(cur, k, 0)
    first = lax.broadcasted_iota(jnp.int32, prev_tail.shape, 0) < k
    head = jnp.where(first, pltpu.roll(prev_tail, k, 0), rolled[:SUBLANES, :])
    return jnp.concatenate([head, rolled[SUBLANES:, :]], axis=0)


def _ffn_up_kernel(xn_ref, wg_ref, wu_ref, cw_ref, cb_ref, o_ref,
                   wgb_ref, wub_ref, carry_ref, *, bm, tiles_per_seq):
    i = pl.program_id(1)

    @pl.when(i == 0)
    def _():
        wgb_ref[...] = wg_ref[...].astype(BF16)
        wub_ref[...] = wu_ref[...].astype(BF16)

    cw = cw_ref[...]
    cb = cb_ref[...]
    @pl.when(i % tiles_per_seq == 0)
    def _():
        carry_ref[...] = jnp.zeros(carry_ref.shape, F32)

    tail = carry_ref[...]
    for r0 in range(0, bm, ROW_SUB):
        rs = slice(r0, r0 + ROW_SUB)
        x = xn_ref[rs, :]
        gate = jnp.dot(x, wgb_ref[...], preferred_element_type=F32)
        up = jnp.dot(x, wub_ref[...], preferred_element_type=F32)
        g1 = _shift_rows(gate, tail, 1)
        g2 = _shift_rows(gate, tail, 2)
        c = cw[0:1, :] * g2 + cw[1:2, :] * g1 + cw[2:3, :] * gate + cb
        o_ref[rs, :] = (c * _sigmoid(c) * up).astype(BF16)
        tail = gate[ROW_SUB - SUBLANES:, :]
    carry_ref[...] = tail


def _ffn_up(xn, w_up, layer, conv_w, conv_b, *, seq):
    t, d = xn.shape
    d_ff = conv_w.shape[1]
    bm, bn = FFN_UP_BM, FFN_UP_BN
    nj = d_ff // bn
    kern = functools.partial(_ffn_up_kernel, bm=bm, tiles_per_seq=seq // bm)
    vmem = (2 * bm * d * 2 + 2 * 2 * d * bn * 4 + 2 * d * bn * 2 + 2 * bm * bn * 2
            + 10 * ROW_SUB * bn * 4 * (bm // ROW_SUB))
    return pl.pallas_call(
        kern,
        grid=(nj, t // bm),
        in_specs=[pl.BlockSpec((bm, d), lambda j, i: (i, 0)),
                  pl.BlockSpec((None, d, bn), lambda j, i: (layer, 0, j)),
                  pl.BlockSpec((None, d, bn), lambda j, i: (layer, 0, j + nj)),
                  pl.BlockSpec((CONV_W, bn), lambda j, i: (0, j)),
                  pl.BlockSpec((1, bn), lambda j, i: (0, j))],
        out_specs=pl.BlockSpec((bm, bn), lambda j, i: (i, j)),
        out_shape=jax.ShapeDtypeStruct((t, d_ff), BF16),
        scratch_shapes=[pltpu.VMEM((d, bn), BF16), pltpu.VMEM((d, bn), BF16),
                        pltpu.VMEM((SUBLANES, bn), F32)],
        compiler_params=_params(2, vmem + (4 << 20)),
        name="ffn_up",
    )(xn, w_up, w_up, conv_w, conv_b)


def _ffn_down_kernel(a_ref, w_ref, h_ref, o_ref):
    o_ref[...] = h_ref[...] + jnp.dot(a_ref[...], w_ref[...], preferred_element_type=F32)


def _ffn_down(act, w, h):
    t, d = h.shape
    d_ff = act.shape[1]
    bm = FFN_DOWN_BM
    vmem = w.size * 2 + 2 * bm * d_ff * 2 + 4 * bm * d * 4 + 2 * bm * d * 4
    return pl.pallas_call(
        _ffn_down_kernel,
        grid=(t // bm,),
        in_specs=[pl.BlockSpec((bm, d_ff), lambda i: (i, 0)), _resident(w.shape),
                  pl.BlockSpec((bm, d), lambda i: (i, 0))],
        out_specs=pl.BlockSpec((bm, d), lambda i: (i, 0)),
        out_shape=jax.ShapeDtypeStruct((t, d), F32),
        compiler_params=_params(1, vmem + (4 << 20)),
        name="ffn_down",
    )(act, w, h)


def _ple_kernel(h_ref, g_ref, wgate_ref, p_ref, wple_ref, gfin_ref, o_ref, *, bm, final):
    for r0 in range(0, bm, ROW_SUB):
        rs = slice(r0, r0 + ROW_SUB)
        h = h_ref[rs, :]
        xn = _rms(h, g_ref[...]).astype(BF16)
        gate = _sigmoid(jnp.dot(xn, wgate_ref[...], preferred_element_type=F32))
        emb = jnp.dot(p_ref[rs, :].astype(BF16), wple_ref[...], preferred_element_type=F32)
        out = h + emb * gate
        if final:
            out = _rms(out, gfin_ref[...])
        o_ref[rs, :] = out


def _ple(h, g, w_gate, p, w_ple, g_final, *, final):
    t, d = h.shape
    bm = PLE_BM
    row = lambda n: pl.BlockSpec((bm, n), lambda i: (i, 0))
    vmem = (w_gate.size + w_ple.size) * 2 + 4 * bm * d * 4 + 2 * bm * p.shape[1] * 4 + 5 * bm * d * 4
    return pl.pallas_call(
        functools.partial(_ple_kernel, bm=bm, final=final),
        grid=(t // bm,),
        in_specs=[row(d), _resident(g.shape), _resident(w_gate.shape), row(p.shape[1]),
                  _resident(w_ple.shape), _resident(g_final.shape)],
        out_specs=row(d),
        out_shape=jax.ShapeDtypeStruct((t, d), F32),
        compiler_params=_params(1, vmem + (4 << 20)),
        name="ple",
    )(h, g, w_gate, p, w_ple, g_final)


def _rope_cols(w):
    half = ROPE_DIM // 2
    x1, x2 = w[..., :half], w[..., half:]
    pad = jnp.zeros(w.shape[:-1] + (LANES - ROPE_DIM,), w.dtype)
    return jnp.concatenate([x1, x2, pad], axis=-1), jnp.concatenate([-x2, x1, pad], axis=-1)


def _prep_w_in(w_in, o3):
    plain, rot = _rope_cols(w_in[:, o3:])
    return jnp.concatenate([w_in[:, :o3], plain, rot], axis=1).astype(BF16)


def _prep_w_uq(w_uq):
    k = w_uq.shape[0]
    w = w_uq.reshape(k, N_HEADS, NOPE_DIM + ROPE_DIM)
    plain, rot = _rope_cols(w[:, :, NOPE_DIM:])
    return jnp.concatenate([w[:, :, :NOPE_DIM].reshape(k, -1), plain.reshape(k, -1),
                            rot.reshape(k, -1)], axis=1).astype(BF16)


def _prep_w_ukv(w_ukv):
    k = w_ukv.shape[0]
    w = w_ukv.reshape(k, N_HEADS, NOPE_DIM + V_DIM)
    return jnp.concatenate([w[:, :, :NOPE_DIM].reshape(k, -1), w[:, :, NOPE_DIM:].reshape(k, -1)],
                           axis=1).astype(BF16)


def kernel(x, p, positions, norm_mix_g, w_in, pool_w, pool_scale, q_norm_g, w_uq, kv_norm_g, w_ukv,
           w_out, norm_ffn_g, w_up, conv_w, conv_b, w_down, norm_ple_g, w_ple, w_ple_gate, final_norm_g):
    batch, seq, d = x.shape
    depth = w_in.shape[0]
    t = batch * seq
    d_pool = pool_scale.shape[1]
    q_lora = q_norm_g.shape[1]
    kv_lora = kv_norm_g.shape[1]
    o3 = d_pool + q_lora + kv_lora

    inv_freq = 1.0 / (ROPE_THETA ** (jnp.arange(0, ROPE_DIM, 2, dtype=F32) / ROPE_DIM))
    ang = positions.astype(F32)[..., None] * inv_freq
    zpad = jnp.zeros((batch, seq, LANES - ROPE_DIM), F32)
    cos128 = jnp.concatenate([jnp.cos(ang), jnp.cos(ang), zpad], axis=-1).reshape(t, LANES)
    sin128 = jnp.concatenate([jnp.sin(ang), jnp.sin(ang), zpad], axis=-1).reshape(t, LANES)

    row = lambda a: a.reshape(1, -1)
    h = x.reshape(t, d)
    for i in range(depth):
        ypool, qn, qr, kn, v, kr = _front(
            h, row(norm_mix_g[i]), _prep_w_in(w_in[i], o3), cos128, sin128,
            pool_w[i].astype(BF16), row(pool_scale[i]), row(q_norm_g[i]), _prep_w_uq(w_uq[i]),
            row(kv_norm_g[i]), _prep_w_ukv(w_ukv[i]),
            seq=seq, d_pool=d_pool, q_lora=q_lora, kv_lora=kv_lora)
        yatt = _attention(qn, qr, kn, kr, v, batch=batch, seq=seq)
        h, xn = _out_proj(ypool, yatt, w_out[i].astype(BF16), h, row(norm_ffn_g[i]))
        act = _ffn_up(xn, w_up, i, conv_w[i], row(conv_b[i]), seq=seq)
        h = _ffn_down(act, w_down[i].astype(BF16), h)
        h = _ple(h, row(norm_ple_g[i]), w_ple_gate[i].astype(BF16), p[i].reshape(t, -1),
                 w_ple[i].astype(BF16), row(final_norm_g), final=(i == depth - 1))
    return h.reshape(batch, seq, d)
```

```python
import functools
import math

import jax
import jax.numpy as jnp
from jax import lax
from jax.experimental import pallas as pl
from jax.experimental.pallas import tpu as pltpu

F32 = jnp.float32
BF16 = jnp.bfloat16

POOL_WINDOWS = (2, 4, 8, 16)
N_POOL_GROUPS = len(POOL_WINDOWS)
N_HEADS = 8
NOPE_DIM = 128
ROPE_DIM = 64
V_DIM = 128
ROPE_THETA = 10000.0
CONV_W = 3
EPS = 1e-6

LANES = 128
SUBLANES = 8
V7X_SCOPED_VMEM_BYTES = 60000 * 1024

POOL_HALO = 16

FRONT_BM = 512
ATTN_BK = 256
ATTN_HEADS = 2
OUT_BM = 512
FFN_UP_BM = 1024
FFN_UP_BN = 512
FFN_DOWN_BM = 256
PLE_BM = 512
ROW_SUB = 256


def _rms(x, g):
    ms = jnp.mean(x * x, axis=-1, keepdims=True)
    return (x * lax.rsqrt(ms + EPS)) * g


def _sigmoid(x):
    return 1.0 / (1.0 + jnp.exp(-x))


def _resident(shape):
    nd = len(shape)
    return pl.BlockSpec(shape, lambda *_: (0,) * nd, pipeline_mode=pl.Buffered(1))


def _layer_resident(stacked, layer):
    shape = stacked.shape[1:]
    return pl.BlockSpec((None,) + shape, lambda *_: (layer,) + (0,) * len(shape),
                        pipeline_mode=pl.Buffered(1))


def _slab_bytes(stacked):
    return math.prod(stacked.shape[1:]) * stacked.dtype.itemsize


def _params(n_axes, vmem_bytes):
    return pltpu.CompilerParams(
        dimension_semantics=("arbitrary",) * n_axes,
        vmem_limit_bytes=min(int(vmem_bytes), V7X_SCOPED_VMEM_BYTES),
    )


def _front_kernel(h_ref, g_ref, win_ref, cos_ref, sin_ref, pw_ref, ps_ref,
                  gq_ref, wq_ref, gkv_ref, wkv_ref,
                  ypool_ref, qn_ref, qr_ref, kn_ref, v_ref, kr_ref,
                  pbuf_ref, *, bm, tiles_per_seq, d_pool, q_lora, kv_lora):
    i = pl.program_id(0)
    seq_tile = i % tiles_per_seq
    pool_c = d_pool // N_POOL_GROUPS
    o1 = d_pool
    o2 = o1 + q_lora
    o3 = o2 + kv_lora

    xn = _rms(h_ref[...], g_ref[...]).astype(BF16)
    u = jnp.dot(xn, win_ref[...], preferred_element_type=F32)

    cos = cos_ref[...]
    sin = sin_ref[...]
    kr_ref[...] = (u[:, o3:o3 + LANES] * cos + u[:, o3 + LANES:o3 + 2 * LANES] * sin).astype(BF16)

    cqn = _rms(u[:, o1:o2], gq_ref[...]).astype(BF16)
    q = jnp.dot(cqn, wq_ref[...], preferred_element_type=F32)
    hn = N_HEADS * NOPE_DIM
    hr = N_HEADS * LANES
    qn_ref[...] = q[:, :hn].astype(BF16)
    cos_h = jnp.concatenate([cos] * N_HEADS, axis=1)
    sin_h = jnp.concatenate([sin] * N_HEADS, axis=1)
    qr_ref[...] = (q[:, hn:hn + hr] * cos_h + q[:, hn + hr:hn + 2 * hr] * sin_h).astype(BF16)

    ckvn = _rms(u[:, o2:o3], gkv_ref[...]).astype(BF16)
    kv = jnp.dot(ckvn, wkv_ref[...], preferred_element_type=F32)
    kn_ref[...] = kv[:, :hn].astype(BF16)
    v_ref[...] = kv[:, hn:].astype(BF16)

    @pl.when(seq_tile == 0)
    def _():
        pbuf_ref[0:POOL_HALO, :] = jnp.zeros((POOL_HALO, d_pool), F32)

    pbuf_ref[POOL_HALO:POOL_HALO + bm, :] = u[:, :o1]
    t1 = lax.broadcasted_iota(jnp.int32, (bm, LANES), 0) + (seq_tile * bm + 1)
    for grp, w in enumerate(POOL_WINDOWS):
        c0, c1 = grp * pool_c, (grp + 1) * pool_c
        ext = pbuf_ref[:, c0:c1]
        acc = ext + pltpu.roll(ext, 1, 0)
        span = 2
        while span < w:
            acc = acc + pltpu.roll(acc, span, 0)
            span *= 2
        win = acc[POOL_HALO:, :]
        inv = 1.0 / jnp.minimum(t1, w).astype(F32)
        inv = jnp.concatenate([inv] * (pool_c // LANES), axis=1)
        diff = (win * inv - ext[POOL_HALO:, :]).astype(BF16)
        y = jnp.dot(diff, pw_ref[grp], preferred_element_type=F32) * ps_ref[:, c0:c1]
        ypool_ref[:, c0:c1] = y.astype(BF16)
    pbuf_ref[0:POOL_HALO, :] = pbuf_ref[bm:bm + POOL_HALO, :]


def _front(h, layer, g, win, cos128, sin128, pw, ps, gq, wq, gkv, wkv, *, seq, d_pool, q_lora, kv_lora):
    t, d = h.shape
    bm = FRONT_BM
    hn = N_HEADS * NOPE_DIM
    row = lambda n: pl.BlockSpec((bm, n), lambda i: (i, 0))
    res = lambda a: _layer_resident(a, layer)
    kern = functools.partial(_front_kernel, bm=bm, tiles_per_seq=seq // bm,
                             d_pool=d_pool, q_lora=q_lora, kv_lora=kv_lora)
    weights = _slab_bytes(win) + _slab_bytes(wq) + _slab_bytes(wkv) + _slab_bytes(pw)
    tiles = 2 * bm * (d * 4 + 2 * LANES * 4 + (d_pool + 4 * hn + LANES) * 2)
    temps = bm * (win.shape[-1] + wq.shape[-1] + wkv.shape[-1] + 4 * d_pool) * 4 + bm * d * 6
    return pl.pallas_call(
        kern,
        grid=(t // bm,),
        in_specs=[row(d), res(g), res(win), row(LANES), row(LANES),
                  res(pw), res(ps), res(gq), res(wq), res(gkv), res(wkv)],
        out_specs=[row(d_pool), row(hn), row(hn), row(hn), row(hn), row(LANES)],
        out_shape=[jax.ShapeDtypeStruct((t, d_pool), BF16)] + [jax.ShapeDtypeStruct((t, hn), BF16)] * 4
                  + [jax.ShapeDtypeStruct((t, LANES), BF16)],
        scratch_shapes=[pltpu.VMEM((POOL_HALO + bm, d_pool), F32)],
        compiler_params=_params(1, weights + tiles + temps + (POOL_HALO + bm) * d_pool * 4),
        name="front",
    )(h, g, win, cos128, sin128, pw, ps, gq, wq, gkv, wkv)


def _attn_kernel(qn_ref, qr_ref, kn_ref, kr_ref, v_ref, o_ref,
                 vt_ref, t_ref, p_ref, acc_ref, *, seq, bk, c2, heads):
    bq = 2 * bk
    nt_dims = (((1,), (1,)), ((), ()))

    def krows(c):
        return pl.ds(pl.multiple_of(c * bk, bk), bk)

    def qrows(c):
        return pl.ds(pl.multiple_of(c * bq, bq), bq)

    ones_rows = (lax.broadcasted_iota(jnp.int32, (SUBLANES, bk), 0) == 0).astype(F32)

    def load_vt(c, _):
        for hh in range(heads):
            v = v_ref[krows(c), hh * V_DIM:(hh + 1) * V_DIM]
            vt_ref[hh, c] = jnp.concatenate([v.astype(F32).T, ones_rows], axis=0).astype(BF16)
        return 0

    lax.fori_loop(0, seq // bk, load_vt, 0)

    def scores(q, kj, slot):
        kr = kr_ref[krows(kj), :]
        for hh in range(heads):
            k = jnp.concatenate([kn_ref[krows(kj), hh * NOPE_DIM:(hh + 1) * NOPE_DIM], kr], axis=-1)
            t_ref[slot, hh] = lax.dot_general(k, q[hh], nt_dims, preferred_element_type=F32) * c2

    def softmax(slot, ms, diag_offset=None):
        new_ms, alphas = [], []
        for hh in range(heads):
            m = ms[hh]
            t = t_ref[slot, hh]
            if diag_offset is not None:
                key = lax.broadcasted_iota(jnp.int32, (bk, bq), 0) + diag_offset
                qry = lax.broadcasted_iota(jnp.int32, (bk, bq), 1)
                t = jnp.where(key <= qry, t, -jnp.inf)
            m_new = jnp.maximum(m, jnp.max(t, axis=0, keepdims=True))
            p_ref[slot, hh] = jnp.exp2(t - m_new).astype(BF16)
            new_ms.append(m_new)
            alphas.append(jnp.exp2(m - m_new))
        return tuple(new_ms), tuple(alphas)

    def values(kj, slot, alphas):
        for hh in range(heads):
            acc_ref[hh] = alphas[hh] * acc_ref[hh] + jnp.dot(
                vt_ref[hh, kj], p_ref[slot, hh], preferred_element_type=F32)

    nq = seq // bq

    def load_q(qi):
        return [jnp.concatenate([qn_ref[qrows(qi), hh * NOPE_DIM:(hh + 1) * NOPE_DIM],
                                 qr_ref[qrows(qi), hh * LANES:(hh + 1) * LANES]], axis=-1)
                for hh in range(heads)]

    scores(load_q(0), 0, 0)

    def q_body(qi, _):
        q = load_q(qi)
        acc_ref[...] = jnp.zeros(acc_ref.shape, F32)
        p_ref[1] = jnp.zeros(p_ref.shape[1:], BF16)
        ms = tuple(jnp.full((1, bq), -jnp.inf, F32) for _ in range(heads))
        alphas = tuple(jnp.ones((1, bq), F32) for _ in range(heads))

        def pair(b, carry):
            ms, alphas = carry
            values(jnp.maximum(b - 1, 0), 1, alphas)
            ms, alphas = softmax(0, ms)
            scores(q, b + 1, 1)
            values(b, 0, alphas)
            ms, alphas = softmax(1, ms)
            scores(q, b + 2, 0)
            return ms, alphas

        carry = lax.fori_loop(0, qi // 2, lambda u, c: pair(4 * u + 2, pair(4 * u, c)), (ms, alphas))
        ms, alphas = lax.fori_loop(qi - qi % 2, qi, lambda u, c: pair(2 * u, c), carry)
        b = 2 * qi
        values(jnp.maximum(b - 1, 0), 1, alphas)
        ms, alphas = softmax(0, ms, diag_offset=0)
        scores(q, b + 1, 1)
        scores(load_q(jnp.minimum(qi + 1, nq - 1)), 0, 0)
        values(b, 0, alphas)
        ms, alphas = softmax(1, ms, diag_offset=bk)
        values(b + 1, 1, alphas)
        for hh in range(heads):
            out = acc_ref[hh, :V_DIM, :] / acc_ref[hh, V_DIM:V_DIM + 1, :]
            o_ref[qrows(qi), hh * V_DIM:(hh + 1) * V_DIM] = out.T.astype(BF16)
        return 0

    lax.fori_loop(0, nq, q_body, 0)


def _attention(qn, qr, kn, kr, v, *, batch, seq):
    bk, heads = ATTN_BK, ATTN_HEADS
    bq = 2 * bk
    hn = N_HEADS * NOPE_DIM
    w = heads * LANES
    r3 = lambda a: a.reshape(batch, seq, a.shape[-1])
    hspec = pl.BlockSpec((None, seq, w), lambda b, hp: (b, 0, hp))
    krspec = pl.BlockSpec((None, seq, LANES), lambda b, hp: (b, 0, 0))
    c2 = math.log2(math.e) / math.sqrt(NOPE_DIM + ROPE_DIM)
    kern = functools.partial(_attn_kernel, seq=seq, bk=bk, c2=c2, heads=heads)
    vext = V_DIM + SUBLANES
    scratch = heads * seq * vext * 2 + 2 * heads * bk * bq * (4 + 2) + heads * vext * bq * 4
    vmem = 2 * (5 * seq * w + seq * LANES) * 2 + scratch + heads * 8 * bk * bq * 4
    out = pl.pallas_call(
        kern,
        grid=(batch, N_HEADS // heads),
        in_specs=[hspec, hspec, hspec, krspec, hspec],
        out_specs=hspec,
        out_shape=jax.ShapeDtypeStruct((batch, seq, hn), BF16),
        scratch_shapes=[pltpu.VMEM((heads, seq // bk, vext, bk), BF16),
                        pltpu.VMEM((2, heads, bk, bq), F32),
                        pltpu.VMEM((2, heads, bk, bq), BF16),
                        pltpu.VMEM((heads, vext, bq), F32)],
        compiler_params=_params(2, vmem),
        name="attention",
    )(r3(qn), r3(qr), r3(kn), r3(kr), r3(v))
    return out.reshape(batch * seq, hn)


def _out_proj_kernel(yp_ref, ya_ref, w_ref, h_ref, g_ref, o_ref, xn_ref, *, bm):
    for r0 in range(0, bm, ROW_SUB):
        rs = slice(r0, r0 + ROW_SUB)
        y = jnp.concatenate([yp_ref[rs, :], ya_ref[rs, :]], axis=-1)
        hn = h_ref[rs, :] + jnp.dot(y, w_ref[...], preferred_element_type=F32)
        o_ref[rs, :] = hn
        xn_ref[rs, :] = _rms(hn, g_ref[...]).astype(BF16)


def _out_proj(yp, ya, layer, w, h, g):
    t, d = h.shape
    bm = OUT_BM
    row = lambda n: pl.BlockSpec((bm, n), lambda i: (i, 0))
    vmem = _slab_bytes(w) + 2 * bm * (yp.shape[1] + ya.shape[1]) * 2 + 4 * bm * d * 4 + 2 * bm * d * 2
    return pl.pallas_call(
        functools.partial(_out_proj_kernel, bm=bm),
        grid=(t // bm,),
        in_specs=[row(yp.shape[1]), row(ya.shape[1]), _layer_resident(w, layer), row(d),
                  _layer_resident(g, layer)],
        out_specs=[row(d), row(d)],
        out_shape=[jax.ShapeDtypeStruct((t, d), F32), jax.ShapeDtypeStruct((t, d), BF16)],
        compiler_params=_params(1, vmem + 4 * bm * d * 4 + (4 << 20)),
        name="out_proj",
    )(yp, ya, w, h, g)


def _shift_rows(cur, prev_tail, k):
    rolled = pltpu.roll(cur, k, 0)
    first = lax.broadcasted_iota(jnp.int32, prev_tail.shape, 0) < k
    head = jnp.where(first, pltpu.roll(prev_tail, k, 0), rolled[:SUBLANES, :])
    return jnp.concatenate([head, rolled[SUBLANES:, :]], axis=0)


def _ffn_up_kernel(xn_ref, wg_ref, wu_ref, cw_ref, cb_ref, o_ref,
                   wgb_ref, wub_ref, carry_ref, *, bm, tiles_per_seq):
    i = pl.program_id(1)

    @pl.when(i == 0)
    def _():
        wgb_ref[...] = wg_ref[...].astype(BF16)
        wub_ref[...] = wu_ref[...].astype(BF16)

    cw = cw_ref[...]
    cb = cb_ref[...]
    @pl.when(i % tiles_per_seq == 0)
    def _():
        carry_ref[...] = jnp.zeros(carry_ref.shape, F32)

    tail = carry_ref[...]
    for r0 in range(0, bm, ROW_SUB):
        rs = slice(r0, r0 + ROW_SUB)
        x = xn_ref[rs, :]
        gate = jnp.dot(x, wgb_ref[...], preferred_element_type=F32)
        up = jnp.dot(x, wub_ref[...], preferred_element_type=F32)
        g1 = _shift_rows(gate, tail, 1)
        g2 = _shift_rows(gate, tail, 2)
        c = cw[0:1, :] * g2 + cw[1:2, :] * g1 + cw[2:3, :] * gate + cb
        o_ref[rs, :] = (c * _sigmoid(c) * up).astype(BF16)
        tail = gate[ROW_SUB - SUBLANES:, :]
    carry_ref[...] = tail


def _ffn_up(xn, w_up, layer, conv_w, conv_b, *, seq):
    t, d = xn.shape
    d_ff = conv_w.shape[-1]
    bm, bn = FFN_UP_BM, FFN_UP_BN
    nj = d_ff // bn
    kern = functools.partial(_ffn_up_kernel, bm=bm, tiles_per_seq=seq // bm)
    vmem = (2 * bm * d * 2 + 2 * 2 * d * bn * 4 + 2 * d * bn * 2 + 2 * bm * bn * 2
            + 10 * ROW_SUB * bn * 4 * (bm // ROW_SUB))
    return pl.pallas_call(
        kern,
        grid=(nj, t // bm),
        in_specs=[pl.BlockSpec((bm, d), lambda j, i: (i, 0)),
                  pl.BlockSpec((None, d, bn), lambda j, i: (layer, 0, j)),
                  pl.BlockSpec((None, d, bn), lambda j, i: (layer, 0, j + nj)),
                  pl.BlockSpec((None, CONV_W, bn), lambda j, i: (layer, 0, j)),
                  pl.BlockSpec((None, 1, bn), lambda j, i: (layer, 0, j))],
        out_specs=pl.BlockSpec((bm, bn), lambda j, i: (i, j)),
        out_shape=jax.ShapeDtypeStruct((t, d_ff), BF16),
        scratch_shapes=[pltpu.VMEM((d, bn), BF16), pltpu.VMEM((d, bn), BF16),
                        pltpu.VMEM((SUBLANES, bn), F32)],
        compiler_params=_params(2, vmem + (4 << 20)),
        name="ffn_up",
    )(xn, w_up, w_up, conv_w, conv_b)


def _ffn_down_kernel(a_ref, w_ref, h_ref, o_ref):
    o_ref[...] = h_ref[...] + jnp.dot(a_ref[...], w_ref[...], preferred_element_type=F32)


def _ffn_down(act, layer, w, h):
    t, d = h.shape
    d_ff = act.shape[1]
    bm = FFN_DOWN_BM
    vmem = _slab_bytes(w) + 2 * bm * d_ff * 2 + 4 * bm * d * 4 + 2 * bm * d * 4
    return pl.pallas_call(
        _ffn_down_kernel,
        grid=(t // bm,),
        in_specs=[pl.BlockSpec((bm, d_ff), lambda i: (i, 0)), _layer_resident(w, layer),
                  pl.BlockSpec((bm, d), lambda i: (i, 0))],
        out_specs=pl.BlockSpec((bm, d), lambda i: (i, 0)),
        out_shape=jax.ShapeDtypeStruct((t, d), F32),
        compiler_params=_params(1, vmem + (4 << 20)),
        name="ffn_down",
    )(act, w, h)


def _ple_kernel(h_ref, g_ref, wgate_ref, p_ref, wple_ref, gfin_ref, o_ref, *, bm, final):
    for r0 in range(0, bm, ROW_SUB):
        rs = slice(r0, r0 + ROW_SUB)
        h = h_ref[rs, :]
        xn = _rms(h, g_ref[...]).astype(BF16)
        gate = _sigmoid(jnp.dot(xn, wgate_ref[...], preferred_element_type=F32))
        emb = jnp.dot(p_ref[rs, :].astype(BF16), wple_ref[...], preferred_element_type=F32)
        out = h + emb * gate
        if final:
            out = _rms(out, gfin_ref[...])
        o_ref[rs, :] = out


def _ple(h, layer, g, w_gate, p, w_ple, g_final, *, final):
    t, d = h.shape
    bm = PLE_BM
    row = lambda n: pl.BlockSpec((bm, n), lambda i: (i, 0))
    ple_dim = p.shape[-1]
    vmem = (_slab_bytes(w_gate) + _slab_bytes(w_ple) + 4 * bm * d * 4 + 2 * bm * ple_dim * 4
            + 5 * bm * d * 4)
    return pl.pallas_call(
        functools.partial(_ple_kernel, bm=bm, final=final),
        grid=(t // bm,),
        in_specs=[row(d), _layer_resident(g, layer), _layer_resident(w_gate, layer),
                  pl.BlockSpec((None, bm, ple_dim), lambda i: (layer, i, 0)),
                  _layer_resident(w_ple, layer), _resident(g_final.shape)],
        out_specs=row(d),
        out_shape=jax.ShapeDtypeStruct((t, d), F32),
        compiler_params=_params(1, vmem + (4 << 20)),
        name="ple",
    )(h, g, w_gate, p, w_ple, g_final)


def _rope_cols(w):
    half = ROPE_DIM // 2
    x1, x2 = w[..., :half], w[..., half:]
    pad = jnp.zeros(w.shape[:-1] + (LANES - ROPE_DIM,), w.dtype)
    return jnp.concatenate([x1, x2, pad], axis=-1), jnp.concatenate([-x2, x1, pad], axis=-1)


def _prep_w_in(w_in, o3):
    plain, rot = _rope_cols(w_in[..., o3:].astype(BF16))
    return jnp.concatenate([w_in[..., :o3].astype(BF16), plain, rot], axis=-1)


def _prep_w_uq(w_uq):
    lead = w_uq.shape[:-1]
    w = w_uq.astype(BF16).reshape(lead + (N_HEADS, NOPE_DIM + ROPE_DIM))
    plain, rot = _rope_cols(w[..., NOPE_DIM:])
    flat = lambda a: a.reshape(lead + (-1,))
    return jnp.concatenate([flat(w[..., :NOPE_DIM]), flat(plain), flat(rot)], axis=-1)


def _prep_w_ukv(w_ukv):
    lead = w_ukv.shape[:-1]
    w = w_ukv.astype(BF16).reshape(lead + (N_HEADS, NOPE_DIM + V_DIM))
    flat = lambda a: a.reshape(lead + (-1,))
    return jnp.concatenate([flat(w[..., :NOPE_DIM]), flat(w[..., NOPE_DIM:])], axis=-1)


def kernel(x, p, positions, norm_mix_g, w_in, pool_w, pool_scale, q_norm_g, w_uq, kv_norm_g, w_ukv,
           w_out, norm_ffn_g, w_up, conv_w, conv_b, w_down, norm_ple_g, w_ple, w_ple_gate, final_norm_g):
    batch, seq, d = x.shape
    depth = w_in.shape[0]
    t = batch * seq
    d_pool = pool_scale.shape[1]
    q_lora = q_norm_g.shape[1]
    kv_lora = kv_norm_g.shape[1]
    o3 = d_pool + q_lora + kv_lora

    inv_freq = 1.0 / (ROPE_THETA ** (jnp.arange(0, ROPE_DIM, 2, dtype=F32) / ROPE_DIM))
    ang = positions.astype(F32)[..., None] * inv_freq
    zpad = jnp.zeros((batch, seq, LANES - ROPE_DIM), F32)
    cos128 = jnp.concatenate([jnp.cos(ang), jnp.cos(ang), zpad], axis=-1).reshape(t, LANES)
    sin128 = jnp.concatenate([jnp.sin(ang), jnp.sin(ang), zpad], axis=-1).reshape(t, LANES)

    rows = lambda a: a.reshape(depth, 1, -1)
    win_b, wq_b, wkv_b = _prep_w_in(w_in, o3), _prep_w_uq(w_uq), _prep_w_ukv(w_ukv)
    pw_b, wout_b, wdown_b = pool_w.astype(BF16), w_out.astype(BF16), w_down.astype(BF16)
    wgate_b, wple_b = w_ple_gate.astype(BF16), w_ple.astype(BF16)
    g_mix, g_q, g_kv, g_ffn, g_ple = (rows(a) for a in (norm_mix_g, q_norm_g, kv_norm_g,
                                                        norm_ffn_g, norm_ple_g))
    ps, cb = rows(pool_scale), rows(conv_b)
    p3 = p.reshape(depth, t, p.shape[-1])

    h = x.reshape(t, d)
    for i in range(depth):
        ypool, qn, qr, kn, v, kr = _front(
            h, i, g_mix, win_b, cos128, sin128, pw_b, ps, g_q, wq_b, g_kv, wkv_b,
            seq=seq, d_pool=d_pool, q_lora=q_lora, kv_lora=kv_lora)
        yatt = _attention(qn, qr, kn, kr, v, batch=batch, seq=seq)
        h, xn = _out_proj(ypool, yatt, i, wout_b, h, g_ffn)
        act = _ffn_up(xn, w_up, i, conv_w, cb, seq=seq)
        h = _ffn_down(act, i, wdown_b, h)
        h = _ple(h, i, g_ple, wgate_b, p3, wple_b, final_norm_g.reshape(1, -1),
                 final=(i == depth - 1))
    return h.reshape(batch, seq, d)
```

```python
import functools
import math

import jax
import jax.numpy as jnp
from jax import lax
from jax.experimental import pallas as pl
from jax.experimental.pallas import tpu as pltpu

F32 = jnp.float32
BF16 = jnp.bfloat16

POOL_WINDOWS = (2, 4, 8, 16)
N_POOL_GROUPS = len(POOL_WINDOWS)
N_HEADS = 8
NOPE_DIM = 128
ROPE_DIM = 64
V_DIM = 128
ROPE_THETA = 10000.0
ATTN_C2 = math.log2(math.e) / math.sqrt(NOPE_DIM + ROPE_DIM)
CONV_W = 3
EPS = 1e-6

LANES = 128
SUBLANES = 8
V7X_SCOPED_VMEM_BYTES = 60000 * 1024

POOL_HALO = 16

FRONT_BM = 512
ATTN_BK = 256
ATTN_HEADS = 2
OUT_BM = 512
FFN_UP_BM = 1024
FFN_UP_SUB = 256
FFN_UP_BN = 512
FFN_DOWN_BM = 256
PLE_BM = 512
ROW_SUB = 256


def _rms(x, g):
    ms = jnp.mean(x * x, axis=-1, keepdims=True)
    return (x * lax.rsqrt(ms + EPS)) * g


def _sigmoid(x):
    return 1.0 / (1.0 + jnp.exp(-x))


def _resident(shape):
    nd = len(shape)
    return pl.BlockSpec(shape, lambda *_: (0,) * nd, pipeline_mode=pl.Buffered(1))


def _layer_resident(stacked, layer):
    shape = stacked.shape[1:]
    return pl.BlockSpec((None,) + shape, lambda *_: (layer,) + (0,) * len(shape),
                        pipeline_mode=pl.Buffered(1))


def _slab_bytes(stacked):
    return math.prod(stacked.shape[1:]) * stacked.dtype.itemsize


def _params(n_axes, vmem_bytes):
    return pltpu.CompilerParams(
        dimension_semantics=("arbitrary",) * n_axes,
        vmem_limit_bytes=min(int(vmem_bytes), V7X_SCOPED_VMEM_BYTES),
    )


def _front_kernel(h_ref, g_ref, win_ref, cos_ref, sin_ref, pw_ref, ps_ref,
                  gq_ref, wq_ref, gkv_ref, wkv_ref,
                  ypool_ref, qn_ref, qr_ref, kn_ref, v_ref, kr_ref,
                  carry_ref, *, bm, tiles_per_seq, d_pool, q_lora, kv_lora):
    i = pl.program_id(0)
    seq_tile = i % tiles_per_seq
    pool_c = d_pool // N_POOL_GROUPS
    o1 = d_pool
    o2 = o1 + q_lora
    o3 = o2 + kv_lora
    hn = N_HEADS * NOPE_DIM
    hr = N_HEADS * LANES

    @pl.when(seq_tile == 0)
    def _():
        carry_ref[...] = jnp.zeros(carry_ref.shape, F32)

    tail = carry_ref[...]
    for r0 in range(0, bm, ROW_SUB):
        rs = slice(r0, r0 + ROW_SUB)
        xn = _rms(h_ref[rs, :], g_ref[...]).astype(BF16)
        u = jnp.dot(xn, win_ref[...], preferred_element_type=F32)

        cos = cos_ref[rs, :]
        sin = sin_ref[rs, :]
        kr_ref[rs, :] = (u[:, o3:o3 + LANES] * cos + u[:, o3 + LANES:o3 + 2 * LANES] * sin).astype(BF16)

        cqn = _rms(u[:, o1:o2], gq_ref[...]).astype(BF16)
        q = jnp.dot(cqn, wq_ref[...], preferred_element_type=F32)
        qn_ref[rs, :] = (q[:, :hn] * ATTN_C2).astype(BF16)
        cos_h = jnp.concatenate([cos] * N_HEADS, axis=1)
        sin_h = jnp.concatenate([sin] * N_HEADS, axis=1)
        q_rope = q[:, hn:hn + hr] * cos_h + q[:, hn + hr:hn + 2 * hr] * sin_h
        qr_ref[rs, :] = (q_rope * ATTN_C2).astype(BF16)

        ckvn = _rms(u[:, o2:o3], gkv_ref[...]).astype(BF16)
        kv = jnp.dot(ckvn, wkv_ref[...], preferred_element_type=F32)
        kn_ref[rs, :] = kv[:, :hn].astype(BF16)
        v_ref[rs, :] = kv[:, hn:].astype(BF16)

        up = u[:, :o1]
        t1 = lax.broadcasted_iota(jnp.int32, (ROW_SUB, LANES), 0) + (seq_tile * bm + r0 + 1)
        for grp, w in enumerate(POOL_WINDOWS):
            c0, c1 = grp * pool_c, (grp + 1) * pool_c
            ext = jnp.concatenate([tail[:, c0:c1], up[:, c0:c1]], axis=0)
            acc = ext + pltpu.roll(ext, 1, 0)
            span = 2
            while span < w:
                acc = acc + pltpu.roll(acc, span, 0)
                span *= 2
            inv = 1.0 / jnp.minimum(t1, w).astype(F32)
            inv = jnp.concatenate([inv] * (pool_c // LANES), axis=1)
            diff = (acc[POOL_HALO:, :] * inv - up[:, c0:c1]).astype(BF16)
            y = jnp.dot(diff, pw_ref[grp], preferred_element_type=F32) * ps_ref[:, c0:c1]
            ypool_ref[rs, c0:c1] = y.astype(BF16)
        tail = up[ROW_SUB - POOL_HALO:, :]
    carry_ref[...] = tail


def _front(h, layer, g, win, cos128, sin128, pw, ps, gq, wq, gkv, wkv, *, seq, d_pool, q_lora, kv_lora):
    t, d = h.shape
    bm = FRONT_BM
    hn = N_HEADS * NOPE_DIM
    row = lambda n: pl.BlockSpec((bm, n), lambda i: (i, 0))
    res = lambda a: _layer_resident(a, layer)
    kern = functools.partial(_front_kernel, bm=bm, tiles_per_seq=seq // bm,
                             d_pool=d_pool, q_lora=q_lora, kv_lora=kv_lora)
    weights = _slab_bytes(win) + _slab_bytes(wq) + _slab_bytes(wkv) + _slab_bytes(pw)
    tiles = 2 * bm * (d * 4 + 2 * LANES * 4 + (d_pool + 4 * hn + LANES) * 2)
    temps = bm * (win.shape[-1] + wq.shape[-1] + wkv.shape[-1] + 4 * d_pool) * 4 + bm * d * 6
    return pl.pallas_call(
        kern,
        grid=(t // bm,),
        in_specs=[row(d), res(g), res(win), row(LANES), row(LANES),
                  res(pw), res(ps), res(gq), res(wq), res(gkv), res(wkv)],
        out_specs=[row(d_pool), row(hn), row(hn), row(hn), row(hn), row(LANES)],
        out_shape=[jax.ShapeDtypeStruct((t, d_pool), BF16)] + [jax.ShapeDtypeStruct((t, hn), BF16)] * 4
                  + [jax.ShapeDtypeStruct((t, LANES), BF16)],
        scratch_shapes=[pltpu.VMEM((POOL_HALO, d_pool), F32)],
        compiler_params=_params(1, weights + tiles + temps),
        name="front",
    )(h, g, win, cos128, sin128, pw, ps, gq, wq, gkv, wkv)


def _attn_kernel(qn_ref, qr_ref, kn_ref, kr_ref, v_ref, o_ref,
                 vt_ref, t_ref, p_ref, acc_ref, *, seq, bk, heads):
    bq = 2 * bk
    nt_dims = (((1,), (1,)), ((), ()))

    def krows(c):
        return pl.ds(pl.multiple_of(c * bk, bk), bk)

    def qrows(c):
        return pl.ds(pl.multiple_of(c * bq, bq), bq)

    ones_rows = (lax.broadcasted_iota(jnp.int32, (SUBLANES, bk), 0) == 0).astype(F32)

    def load_vt(c, _):
        for hh in range(heads):
            v = v_ref[krows(c), hh * V_DIM:(hh + 1) * V_DIM]
            vt_ref[hh, c] = jnp.concatenate([v.astype(F32).T, ones_rows], axis=0).astype(BF16)
        return 0

    lax.fori_loop(0, seq // bk, load_vt, 0)

    def scores(q, kj, slot):
        kr = kr_ref[krows(kj), :]
        for hh in range(heads):
            k = jnp.concatenate([kn_ref[krows(kj), hh * NOPE_DIM:(hh + 1) * NOPE_DIM], kr], axis=-1)
            t_ref[slot, hh] = lax.dot_general(k, q[hh], nt_dims, preferred_element_type=F32)

    def softmax(slot, ms, diag_offset=None):
        new_ms, alphas = [], []
        for hh in range(heads):
            m = ms[hh]
            t = t_ref[slot, hh]
            if diag_offset is not None:
                key = lax.broadcasted_iota(jnp.int32, (bk, bq), 0) + diag_offset
                qry = lax.broadcasted_iota(jnp.int32, (bk, bq), 1)
                t = jnp.where(key <= qry, t, -jnp.inf)
            m_new = jnp.maximum(m, jnp.max(t, axis=0, keepdims=True))
            p_ref[slot, hh] = jnp.exp2(t - m_new).astype(BF16)
            new_ms.append(m_new)
            alphas.append(jnp.exp2(m - m_new))
        return tuple(new_ms), tuple(alphas)

    def values(kj, slot, alphas):
        for hh in range(heads):
            acc_ref[hh] = alphas[hh] * acc_ref[hh] + jnp.dot(
                vt_ref[hh, kj], p_ref[slot, hh], preferred_element_type=F32)

    nq = seq // bq

    def load_q(qi):
        return [jnp.concatenate([qn_ref[qrows(qi), hh * NOPE_DIM:(hh + 1) * NOPE_DIM],
                                 qr_ref[qrows(qi), hh * LANES:(hh + 1) * LANES]], axis=-1)
                for hh in range(heads)]

    scores(load_q(0), 0, 0)

    def q_body(qi, _):
        q = load_q(qi)
        acc_ref[...] = jnp.zeros(acc_ref.shape, F32)
        p_ref[1] = jnp.zeros(p_ref.shape[1:], BF16)
        ms = tuple(jnp.full((1, bq), -jnp.inf, F32) for _ in range(heads))
        alphas = tuple(jnp.ones((1, bq), F32) for _ in range(heads))

        def pair(b, carry):
            ms, alphas = carry
            values(jnp.maximum(b - 1, 0), 1, alphas)
            ms, alphas = softmax(0, ms)
            scores(q, b + 1, 1)
            values(b, 0, alphas)
            ms, alphas = softmax(1, ms)
            scores(q, b + 2, 0)
            return ms, alphas

        carry = lax.fori_loop(0, qi // 2, lambda u, c: pair(4 * u + 2, pair(4 * u, c)), (ms, alphas))
        ms, alphas = lax.fori_loop(qi - qi % 2, qi, lambda u, c: pair(2 * u, c), carry)
        b = 2 * qi
        values(jnp.maximum(b - 1, 0), 1, alphas)
        ms, alphas = softmax(0, ms, diag_offset=0)
        scores(q, b + 1, 1)
        scores(load_q(jnp.minimum(qi + 1, nq - 1)), 0, 0)
        values(b, 0, alphas)
        ms, alphas = softmax(1, ms, diag_offset=bk)
        values(b + 1, 1, alphas)
        for hh in range(heads):
            out = acc_ref[hh, :V_DIM, :] / acc_ref[hh, V_DIM:V_DIM + 1, :]
            o_ref[qrows(qi), hh * V_DIM:(hh + 1) * V_DIM] = out.T.astype(BF16)
        return 0

    lax.fori_loop(0, nq, q_body, 0)


def _attention(qn, qr, kn, kr, v, *, batch, seq):
    bk, heads = ATTN_BK, ATTN_HEADS
    bq = 2 * bk
    hn = N_HEADS * NOPE_DIM
    w = heads * LANES
    r3 = lambda a: a.reshape(batch, seq, a.shape[-1])
    hspec = pl.BlockSpec((None, seq, w), lambda b, hp: (b, 0, hp))
    krspec = pl.BlockSpec((None, seq, LANES), lambda b, hp: (b, 0, 0))
    kern = functools.partial(_attn_kernel, seq=seq, bk=bk, heads=heads)
    vext = V_DIM + SUBLANES
    scratch = heads * seq * vext * 2 + 2 * heads * bk * bq * (4 + 2) + heads * vext * bq * 4
    vmem = 2 * (5 * seq * w + seq * LANES) * 2 + scratch + heads * 8 * bk * bq * 4
    out = pl.pallas_call(
        kern,
        grid=(batch, N_HEADS // heads),
        in_specs=[hspec, hspec, hspec, krspec, hspec],
        out_specs=hspec,
        out_shape=jax.ShapeDtypeStruct((batch, seq, hn), BF16),
        scratch_shapes=[pltpu.VMEM((heads, seq // bk, vext, bk), BF16),
                        pltpu.VMEM((2, heads, bk, bq), F32),
                        pltpu.VMEM((2, heads, bk, bq), BF16),
                        pltpu.VMEM((heads, vext, bq), F32)],
        compiler_params=_params(2, vmem),
        name="attention",
    )(r3(qn), r3(qr), r3(kn), r3(kr), r3(v))
    return out.reshape(batch * seq, hn)


def _out_proj_kernel(yp_ref, ya_ref, w_ref, h_ref, g_ref, o_ref, xn_ref, *, bm):
    for r0 in range(0, bm, ROW_SUB):
        rs = slice(r0, r0 + ROW_SUB)
        y = jnp.concatenate([yp_ref[rs, :], ya_ref[rs, :]], axis=-1)
        hn = h_ref[rs, :] + jnp.dot(y, w_ref[...], preferred_element_type=F32)
        o_ref[rs, :] = hn
        xn_ref[rs, :] = _rms(hn, g_ref[...]).astype(BF16)


def _out_proj(yp, ya, layer, w, h, g):
    t, d = h.shape
    bm = OUT_BM
    row = lambda n: pl.BlockSpec((bm, n), lambda i: (i, 0))
    vmem = _slab_bytes(w) + 2 * bm * (yp.shape[1] + ya.shape[1]) * 2 + 4 * bm * d * 4 + 2 * bm * d * 2
    return pl.pallas_call(
        functools.partial(_out_proj_kernel, bm=bm),
        grid=(t // bm,),
        in_specs=[row(yp.shape[1]), row(ya.shape[1]), _layer_resident(w, layer), row(d),
                  _layer_resident(g, layer)],
        out_specs=[row(d), row(d)],
        out_shape=[jax.ShapeDtypeStruct((t, d), F32), jax.ShapeDtypeStruct((t, d), BF16)],
        compiler_params=_params(1, vmem + 4 * bm * d * 4 + (4 << 20)),
        name="out_proj",
    )(yp, ya, w, h, g)


def _shift_rows(cur, prev_tail, k):
    rolled = pltpu.roll(cur, k, 0)
    first = lax.broadcasted_iota(jnp.int32, prev_tail.shape, 0) < k
    head = jnp.where(first, pltpu.roll(prev_tail, k, 0), rolled[:SUBLANES, :])
    return jnp.concatenate([head, rolled[SUBLANES:, :]], axis=0)


def _ffn_up_kernel(xn_ref, wg_ref, wu_ref, cw_ref, cb_ref, o_ref,
                   wgb_ref, wub_ref, carry_ref, *, bm, tiles_per_seq):
    i = pl.program_id(1)

    @pl.when(i == 0)
    def _():
        wgb_ref[...] = wg_ref[...].astype(BF16)
        wub_ref[...] = wu_ref[...].astype(BF16)

    cw = cw_ref[...]
    cb = cb_ref[...]
    @pl.when(i % tiles_per_seq == 0)
    def _():
        carry_ref[...] = jnp.zeros(carry_ref.shape, F32)

    tail = carry_ref[...]
    for r0 in range(0, bm, FFN_UP_SUB):
        rs = slice(r0, r0 + FFN_UP_SUB)
        x = xn_ref[rs, :]
        gate = jnp.dot(x, wgb_ref[...], preferred_element_type=F32)
        up = jnp.dot(x, wub_ref[...], preferred_element_type=F32)
        g1 = _shift_rows(gate, tail, 1)
        g2 = _shift_rows(gate, tail, 2)
        c = cw[0:1, :] * g2 + cw[1:2, :] * g1 + cw[2:3, :] * gate + cb
        o_ref[rs, :] = (c * _sigmoid(c) * up).astype(BF16)
        tail = gate[FFN_UP_SUB - SUBLANES:, :]
    carry_ref[...] = tail


def _ffn_up(xn, w_up, layer, conv_w, conv_b, *, seq):
    t, d = xn.shape
    d_ff = conv_w.shape[-1]
    bm, bn = FFN_UP_BM, FFN_UP_BN
    nj = d_ff // bn
    kern = functools.partial(_ffn_up_kernel, bm=bm, tiles_per_seq=seq // bm)
    vmem = (2 * bm * d * 2 + 2 * 2 * d * bn * 4 + 2 * d * bn * 2 + 2 * bm * bn * 2
            + 10 * FFN_UP_SUB * bn * 4)
    return pl.pallas_call(
        kern,
        grid=(nj, t // bm),
        in_specs=[pl.BlockSpec((bm, d), lambda j, i: (i, 0)),
                  pl.BlockSpec((None, d, bn), lambda j, i: (layer, 0, j)),
                  pl.BlockSpec((None, d, bn), lambda j, i: (layer, 0, j + nj)),
                  pl.BlockSpec((None, CONV_W, bn), lambda j, i: (layer, 0, j)),
                  pl.BlockSpec((None, 1, bn), lambda j, i: (layer, 0, j))],
        out_specs=pl.BlockSpec((bm, bn), lambda j, i: (i, j)),
        out_shape=jax.ShapeDtypeStruct((t, d_ff), BF16),
        scratch_shapes=[pltpu.VMEM((d, bn), BF16), pltpu.VMEM((d, bn), BF16),
                        pltpu.VMEM((SUBLANES, bn), F32)],
        compiler_params=_params(2, vmem + (4 << 20)),
        name="ffn_up",
    )(xn, w_up, w_up, conv_w, conv_b)


def _ffn_down_kernel(a_ref, w_ref, h_ref, o_ref):
    o_ref[...] = h_ref[...] + jnp.dot(a_ref[...], w_ref[...], preferred_element_type=F32)


def _ffn_down(act, layer, w, h):
    t, d = h.shape
    d_ff = act.shape[1]
    bm = FFN_DOWN_BM
    vmem = _slab_bytes(w) + 2 * bm * d_ff * 2 + 4 * bm * d * 4 + 2 * bm * d * 4
    return pl.pallas_call(
        _ffn_down_kernel,
        grid=(t // bm,),
        in_specs=[pl.BlockSpec((bm, d_ff), lambda i: (i, 0)), _layer_resident(w, layer),
                  pl.BlockSpec((bm, d), lambda i: (i, 0))],
        out_specs=pl.BlockSpec((bm, d), lambda i: (i, 0)),
        out_shape=jax.ShapeDtypeStruct((t, d), F32),
        compiler_params=_params(1, vmem + (4 << 20)),
        name="ffn_down",
    )(act, w, h)


def _ple_kernel(h_ref, g_ref, wgate_ref, p_ref, wple_ref, gfin_ref, o_ref, *, bm, final):
    for r0 in range(0, bm, ROW_SUB):
        rs = slice(r0, r0 + ROW_SUB)
        h = h_ref[rs, :]
        xn = _rms(h, g_ref[...]).astype(BF16)
        gate = _sigmoid(jnp.dot(xn, wgate_ref[...], preferred_element_type=F32))
        emb = jnp.dot(p_ref[rs, :].astype(BF16), wple_ref[...], preferred_element_type=F32)
        out = h + emb * gate
        if final:
            out = _rms(out, gfin_ref[...])
        o_ref[rs, :] = out


def _ple(h, layer, g, w_gate, p, w_ple, g_final, *, final):
    t, d = h.shape
    bm = PLE_BM
    row = lambda n: pl.BlockSpec((bm, n), lambda i: (i, 0))
    ple_dim = p.shape[-1]
    vmem = (_slab_bytes(w_gate) + _slab_bytes(w_ple) + 4 * bm * d * 4 + 2 * bm * ple_dim * 4
            + 5 * bm * d * 4)
    return pl.pallas_call(
        functools.partial(_ple_kernel, bm=bm, final=final),
        grid=(t // bm,),
        in_specs=[row(d), _layer_resident(g, layer), _layer_resident(w_gate, layer),
                  pl.BlockSpec((None, bm, ple_dim), lambda i: (layer, i, 0)),
                  _layer_resident(w_ple, layer), _resident(g_final.shape)],
        out_specs=row(d),
        out_shape=jax.ShapeDtypeStruct((t, d), F32),
        compiler_params=_params(1, vmem + (4 << 20)),
        name="ple",
    )(h, g, w_gate, p, w_ple, g_final)


def _rope_cols(w):
    half = ROPE_DIM // 2
    x1, x2 = w[..., :half], w[..., half:]
    pad = jnp.zeros(w.shape[:-1] + (LANES - ROPE_DIM,), w.dtype)
    return jnp.concatenate([x1, x2, pad], axis=-1), jnp.concatenate([-x2, x1, pad], axis=-1)


def _prep_w_in(w_in, o3):
    plain, rot = _rope_cols(w_in[..., o3:].astype(BF16))
    return jnp.concatenate([w_in[..., :o3].astype(BF16), plain, rot], axis=-1)


def _prep_w_uq(w_uq):
    w = w_uq.astype(BF16)
    hd = NOPE_DIM + ROPE_DIM
    head = lambda h, a, b: w[..., h * hd + a:h * hd + b]
    nope = [head(h, 0, NOPE_DIM) for h in range(N_HEADS)]
    ropes = [_rope_cols(head(h, NOPE_DIM, hd)) for h in range(N_HEADS)]
    return jnp.concatenate(nope + [r[0] for r in ropes] + [r[1] for r in ropes], axis=-1)


def _prep_w_ukv(w_ukv):
    w = w_ukv.astype(BF16)
    hd = NOPE_DIM + V_DIM
    k_nope = [w[..., h * hd:h * hd + NOPE_DIM] for h in range(N_HEADS)]
    v = [w[..., h * hd + NOPE_DIM:(h + 1) * hd] for h in range(N_HEADS)]
    return jnp.concatenate(k_nope + v, axis=-1)


def kernel(x, p, positions, norm_mix_g, w_in, pool_w, pool_scale, q_norm_g, w_uq, kv_norm_g, w_ukv,
           w_out, norm_ffn_g, w_up, conv_w, conv_b, w_down, norm_ple_g, w_ple, w_ple_gate, final_norm_g):
    batch, seq, d = x.shape
    depth = w_in.shape[0]
    t = batch * seq
    d_pool = pool_scale.shape[1]
    q_lora = q_norm_g.shape[1]
    kv_lora = kv_norm_g.shape[1]
    o3 = d_pool + q_lora + kv_lora

    inv_freq = 1.0 / (ROPE_THETA ** (jnp.arange(0, ROPE_DIM, 2, dtype=F32) / ROPE_DIM))
    ang = positions.astype(F32)[..., None] * inv_freq
    zpad = jnp.zeros((batch, seq, LANES - ROPE_DIM), F32)
    cos128 = jnp.concatenate([jnp.cos(ang), jnp.cos(ang), zpad], axis=-1).reshape(t, LANES)
    sin128 = jnp.concatenate([jnp.sin(ang), jnp.sin(ang), zpad], axis=-1).reshape(t, LANES)

    rows = lambda a: a.reshape(depth, 1, -1)
    win_b, wq_b, wkv_b = _prep_w_in(w_in, o3), _prep_w_uq(w_uq), _prep_w_ukv(w_ukv)
    pw_b, wout_b, wdown_b = pool_w.astype(BF16), w_out.astype(BF16), w_down.astype(BF16)
    wgate_b, wple_b = w_ple_gate.astype(BF16), w_ple.astype(BF16)
    g_mix, g_q, g_kv, g_ffn, g_ple = (rows(a) for a in (norm_mix_g, q_norm_g, kv_norm_g,
                                                        norm_ffn_g, norm_ple_g))
    ps, cb = rows(pool_scale), rows(conv_b)
    p3 = p.reshape(depth, t, p.shape[-1])

    h = x.reshape(t, d)
    for i in range(depth):
        ypool, qn, qr, kn, v, kr = _front(
            h, i, g_mix, win_b, cos128, sin128, pw_b, ps, g_q, wq_b, g_kv, wkv_b,
            seq=seq, d_pool=d_pool, q_lora=q_lora, kv_lora=kv_lora)
        yatt = _attention(qn, qr, kn, kr, v, batch=batch, seq=seq)
        h, xn = _out_proj(ypool, yatt, i, wout_b, h, g_ffn)
        act = _ffn_up(xn, w_up, i, conv_w, cb, seq=seq)
        h = _ffn_down(act, i, wdown_b, h)
        h = _ple(h, i, g_ple, wgate_b, p3, wple_b, final_norm_g.reshape(1, -1),
                 final=(i == depth - 1))
    return h.reshape(batch, seq, d)
```

```python
import functools
import math

import jax
import jax.numpy as jnp
from jax import lax
from jax.experimental import pallas as pl
from jax.experimental.pallas import tpu as pltpu

F32 = jnp.float32
BF16 = jnp.bfloat16

POOL_WINDOWS = (2, 4, 8, 16)
N_POOL_GROUPS = len(POOL_WINDOWS)
N_HEADS = 8
NOPE_DIM = 128
ROPE_DIM = 64
V_DIM = 128
ROPE_THETA = 10000.0
ATTN_C2 = math.log2(math.e) / math.sqrt(NOPE_DIM + ROPE_DIM)
CONV_W = 3
EPS = 1e-6

LANES = 128
SUBLANES = 8
V7X_SCOPED_VMEM_BYTES = 60000 * 1024

POOL_HALO = 16

FRONT_BM = 512
ATTN_BK = 256
ATTN_HEADS = 2
OUT_BM = 512
FFN_UP_BM = 1024
FFN_UP_SUB = 256
FFN_UP_BN = 512
FFN_DOWN_BM = 256
PLE_BM = 512
ROW_SUB = 256


def _rms(x, g):
    ms = jnp.mean(x * x, axis=-1, keepdims=True)
    return (x * lax.rsqrt(ms + EPS)) * g


def _sigmoid(x):
    return 1.0 / (1.0 + jnp.exp(-x))


def _resident(shape):
    nd = len(shape)
    return pl.BlockSpec(shape, lambda *_: (0,) * nd, pipeline_mode=pl.Buffered(1))


def _layer_resident(stacked, layer):
    shape = stacked.shape[1:]
    return pl.BlockSpec((None,) + shape, lambda *_: (layer,) + (0,) * len(shape),
                        pipeline_mode=pl.Buffered(1))


def _cast_stream(stacked, layer, n_steps, step_of):
    rows, cols = stacked.shape[1:]
    chunk = rows // n_steps
    assert chunk * n_steps == rows and chunk % (2 * SUBLANES) == 0, (rows, n_steps)
    in_spec = pl.BlockSpec((None, chunk, cols), lambda *idx: (layer, step_of(*idx), 0))
    out_spec = pl.BlockSpec((chunk, cols), lambda *idx: (step_of(*idx), 0))
    return in_spec, out_spec, jax.ShapeDtypeStruct((rows, cols), BF16), chunk * cols * (4 + 2) * 2


def _slab_bytes(stacked):
    return math.prod(stacked.shape[1:]) * stacked.dtype.itemsize


def _params(n_axes, vmem_bytes):
    return pltpu.CompilerParams(
        dimension_semantics=("arbitrary",) * n_axes,
        vmem_limit_bytes=min(int(vmem_bytes), V7X_SCOPED_VMEM_BYTES),
    )


def _front_kernel(h_ref, g_ref, win_ref, cos_ref, sin_ref, pw_ref, ps_ref,
                  gq_ref, wq_ref, gkv_ref, wkv_ref, csrc_ref,
                  ypool_ref, qn_ref, qr_ref, kn_ref, v_ref, kr_ref, cdst_ref,
                  carry_ref, *, bm, tiles_per_seq, d_pool, q_lora, kv_lora):
    cdst_ref[...] = csrc_ref[...].astype(BF16)
    i = pl.program_id(0)
    seq_tile = i % tiles_per_seq
    pool_c = d_pool // N_POOL_GROUPS
    o1 = d_pool
    o2 = o1 + q_lora
    o3 = o2 + kv_lora
    hn = N_HEADS * NOPE_DIM
    hr = N_HEADS * LANES

    @pl.when(seq_tile == 0)
    def _():
        carry_ref[...] = jnp.zeros(carry_ref.shape, F32)

    tail = carry_ref[...]
    for r0 in range(0, bm, ROW_SUB):
        rs = slice(r0, r0 + ROW_SUB)
        xn = _rms(h_ref[rs, :], g_ref[...]).astype(BF16)
        u = jnp.dot(xn, win_ref[...], preferred_element_type=F32)

        cos = cos_ref[rs, :]
        sin = sin_ref[rs, :]
        kr_ref[rs, :] = (u[:, o3:o3 + LANES] * cos + u[:, o3 + LANES:o3 + 2 * LANES] * sin).astype(BF16)

        cqn = _rms(u[:, o1:o2], gq_ref[...]).astype(BF16)
        q = jnp.dot(cqn, wq_ref[...], preferred_element_type=F32)
        qn_ref[rs, :] = (q[:, :hn] * ATTN_C2).astype(BF16)
        cos_h = jnp.concatenate([cos] * N_HEADS, axis=1)
        sin_h = jnp.concatenate([sin] * N_HEADS, axis=1)
        q_rope = q[:, hn:hn + hr] * cos_h + q[:, hn + hr:hn + 2 * hr] * sin_h
        qr_ref[rs, :] = (q_rope * ATTN_C2).astype(BF16)

        ckvn = _rms(u[:, o2:o3], gkv_ref[...]).astype(BF16)
        kv = jnp.dot(ckvn, wkv_ref[...], preferred_element_type=F32)
        kn_ref[rs, :] = kv[:, :hn].astype(BF16)
        v_ref[rs, :] = kv[:, hn:].astype(BF16)

        up = u[:, :o1]
        t1 = lax.broadcasted_iota(jnp.int32, (ROW_SUB, LANES), 0) + (seq_tile * bm + r0 + 1)
        for grp, w in enumerate(POOL_WINDOWS):
            c0, c1 = grp * pool_c, (grp + 1) * pool_c
            ext = jnp.concatenate([tail[:, c0:c1], up[:, c0:c1]], axis=0)
            acc = ext + pltpu.roll(ext, 1, 0)
            span = 2
            while span < w:
                acc = acc + pltpu.roll(acc, span, 0)
                span *= 2
            inv = 1.0 / jnp.minimum(t1, w).astype(F32)
            inv = jnp.concatenate([inv] * (pool_c // LANES), axis=1)
            diff = (acc[POOL_HALO:, :] * inv - up[:, c0:c1]).astype(BF16)
            y = jnp.dot(diff, pw_ref[grp], preferred_element_type=F32) * ps_ref[:, c0:c1]
            ypool_ref[rs, c0:c1] = y.astype(BF16)
        tail = up[ROW_SUB - POOL_HALO:, :]
    carry_ref[...] = tail


def _front(h, layer, g, win, cos128, sin128, pw, ps, gq, wq, gkv, wkv, w_cast, *,
           seq, d_pool, q_lora, kv_lora):
    t, d = h.shape
    bm = FRONT_BM
    hn = N_HEADS * NOPE_DIM
    row = lambda n: pl.BlockSpec((bm, n), lambda i: (i, 0))
    res = lambda a: _layer_resident(a, layer)
    kern = functools.partial(_front_kernel, bm=bm, tiles_per_seq=seq // bm,
                             d_pool=d_pool, q_lora=q_lora, kv_lora=kv_lora)
    weights = _slab_bytes(win) + _slab_bytes(wq) + _slab_bytes(wkv) + _slab_bytes(pw)
    tiles = 2 * bm * (d * 4 + 2 * LANES * 4 + (d_pool + 4 * hn + LANES) * 2)
    temps = bm * (win.shape[-1] + wq.shape[-1] + wkv.shape[-1] + 4 * d_pool) * 4 + bm * d * 6
    c_in, c_out, c_shape, c_bytes = _cast_stream(w_cast, layer, t // bm, lambda i: i)
    return pl.pallas_call(
        kern,
        grid=(t // bm,),
        in_specs=[row(d), res(g), res(win), row(LANES), row(LANES),
                  res(pw), res(ps), res(gq), res(wq), res(gkv), res(wkv), c_in],
        out_specs=[row(d_pool), row(hn), row(hn), row(hn), row(hn), row(LANES), c_out],
        out_shape=[jax.ShapeDtypeStruct((t, d_pool), BF16)] + [jax.ShapeDtypeStruct((t, hn), BF16)] * 4
                  + [jax.ShapeDtypeStruct((t, LANES), BF16), c_shape],
        scratch_shapes=[pltpu.VMEM((POOL_HALO, d_pool), F32)],
        compiler_params=_params(1, weights + tiles + temps + c_bytes),
        name="front",
    )(h, g, win, cos128, sin128, pw, ps, gq, wq, gkv, wkv, w_cast)


def _attn_kernel(qn_ref, qr_ref, kn_ref, kr_ref, v_ref, o_ref,
                 vt_ref, t_ref, p_ref, acc_ref, *, seq, bk, heads):
    bq = 2 * bk
    nt_dims = (((1,), (1,)), ((), ()))

    def krows(c):
        return pl.ds(pl.multiple_of(c * bk, bk), bk)

    def qrows(c):
        return pl.ds(pl.multiple_of(c * bq, bq), bq)

    ones_rows = (lax.broadcasted_iota(jnp.int32, (SUBLANES, bk), 0) == 0).astype(F32)

    def load_vt(c, _):
        for hh in range(heads):
            v = v_ref[krows(c), hh * V_DIM:(hh + 1) * V_DIM]
            vt_ref[hh, c] = jnp.concatenate([v.astype(F32).T, ones_rows], axis=0).astype(BF16)
        return 0

    lax.fori_loop(0, seq // bk, load_vt, 0)

    def scores(q, kj, slot):
        kr = kr_ref[krows(kj), :]
        for hh in range(heads):
            k = jnp.concatenate([kn_ref[krows(kj), hh * NOPE_DIM:(hh + 1) * NOPE_DIM], kr], axis=-1)
            t_ref[slot, hh] = lax.dot_general(k, q[hh], nt_dims, preferred_element_type=F32)

    def softmax(slot, ms, diag_offset=None):
        new_ms, alphas = [], []
        for hh in range(heads):
            m = ms[hh]
            t = t_ref[slot, hh]
            if diag_offset is not None:
                key = lax.broadcasted_iota(jnp.int32, (bk, bq), 0) + diag_offset
                qry = lax.broadcasted_iota(jnp.int32, (bk, bq), 1)
                t = jnp.where(key <= qry, t, -jnp.inf)
            m_new = jnp.maximum(m, jnp.max(t, axis=0, keepdims=True))
            p_ref[slot, hh] = jnp.exp2(t - m_new).astype(BF16)
            new_ms.append(m_new)
            alphas.append(jnp.exp2(m - m_new))
        return tuple(new_ms), tuple(alphas)

    def values(kj, slot, alphas):
        for hh in range(heads):
            acc_ref[hh] = alphas[hh] * acc_ref[hh] + jnp.dot(
                vt_ref[hh, kj], p_ref[slot, hh], preferred_element_type=F32)

    nq = seq // bq

    def load_q(qi):
        return [jnp.concatenate([qn_ref[qrows(qi), hh * NOPE_DIM:(hh + 1) * NOPE_DIM],
                                 qr_ref[qrows(qi), hh * LANES:(hh + 1) * LANES]], axis=-1)
                for hh in range(heads)]

    scores(load_q(0), 0, 0)

    def q_body(qi, _):
        q = load_q(qi)
        acc_ref[...] = jnp.zeros(acc_ref.shape, F32)
        p_ref[1] = jnp.zeros(p_ref.shape[1:], BF16)
        ms = tuple(jnp.full((1, bq), -jnp.inf, F32) for _ in range(heads))
        alphas = tuple(jnp.ones((1, bq), F32) for _ in range(heads))

        def pair(b, carry):
            ms, alphas = carry
            values(jnp.maximum(b - 1, 0), 1, alphas)
            ms, alphas = softmax(0, ms)
            scores(q, b + 1, 1)
            values(b, 0, alphas)
            ms, alphas = softmax(1, ms)
            scores(q, b + 2, 0)
            return ms, alphas

        carry = lax.fori_loop(0, qi // 2, lambda u, c: pair(4 * u + 2, pair(4 * u, c)), (ms, alphas))
        ms, alphas = lax.fori_loop(qi - qi % 2, qi, lambda u, c: pair(2 * u, c), carry)
        b = 2 * qi
        values(jnp.maximum(b - 1, 0), 1, alphas)
        ms, alphas = softmax(0, ms, diag_offset=0)
        scores(q, b + 1, 1)
        scores(load_q(jnp.minimum(qi + 1, nq - 1)), 0, 0)
        values(b, 0, alphas)
        ms, alphas = softmax(1, ms, diag_offset=bk)
        values(b + 1, 1, alphas)
        for hh in range(heads):
            out = acc_ref[hh, :V_DIM, :] / acc_ref[hh, V_DIM:V_DIM + 1, :]
            o_ref[qrows(qi), hh * V_DIM:(hh + 1) * V_DIM] = out.T.astype(BF16)
        return 0

    lax.fori_loop(0, nq, q_body, 0)


def _attention(qn, qr, kn, kr, v, *, batch, seq):
    bk, heads = ATTN_BK, ATTN_HEADS
    bq = 2 * bk
    hn = N_HEADS * NOPE_DIM
    w = heads * LANES
    r3 = lambda a: a.reshape(batch, seq, a.shape[-1])
    hspec = pl.BlockSpec((None, seq, w), lambda b, hp: (b, 0, hp))
    krspec = pl.BlockSpec((None, seq, LANES), lambda b, hp: (b, 0, 0))
    kern = functools.partial(_attn_kernel, seq=seq, bk=bk, heads=heads)
    vext = V_DIM + SUBLANES
    scratch = heads * seq * vext * 2 + 2 * heads * bk * bq * (4 + 2) + heads * vext * bq * 4
    vmem = 2 * (5 * seq * w + seq * LANES) * 2 + scratch + heads * 8 * bk * bq * 4
    out = pl.pallas_call(
        kern,
        grid=(batch, N_HEADS // heads),
        in_specs=[hspec, hspec, hspec, krspec, hspec],
        out_specs=hspec,
        out_shape=jax.ShapeDtypeStruct((batch, seq, hn), BF16),
        scratch_shapes=[pltpu.VMEM((heads, seq // bk, vext, bk), BF16),
                        pltpu.VMEM((2, heads, bk, bq), F32),
                        pltpu.VMEM((2, heads, bk, bq), BF16),
                        pltpu.VMEM((heads, vext, bq), F32)],
        compiler_params=_params(2, vmem),
        name="attention",
    )(r3(qn), r3(qr), r3(kn), r3(kr), r3(v))
    return out.reshape(batch * seq, hn)


def _out_proj_kernel(yp_ref, ya_ref, w_ref, h_ref, g_ref, csrc_ref, o_ref, xn_ref, cdst_ref, *, bm):
    cdst_ref[...] = csrc_ref[...].astype(BF16)
    for r0 in range(0, bm, ROW_SUB):
        rs = slice(r0, r0 + ROW_SUB)
        y = jnp.concatenate([yp_ref[rs, :], ya_ref[rs, :]], axis=-1)
        hn = h_ref[rs, :] + jnp.dot(y, w_ref[...], preferred_element_type=F32)
        o_ref[rs, :] = hn
        xn_ref[rs, :] = _rms(hn, g_ref[...]).astype(BF16)


def _out_proj(yp, ya, w, h, layer, g, w_cast):
    t, d = h.shape
    bm = OUT_BM
    row = lambda n: pl.BlockSpec((bm, n), lambda i: (i, 0))
    c_in, c_out, c_shape, c_bytes = _cast_stream(w_cast, layer, t // bm, lambda i: i)
    vmem = w.size * 2 + 2 * bm * (yp.shape[1] + ya.shape[1]) * 2 + 4 * bm * d * 4 + 2 * bm * d * 2
    return pl.pallas_call(
        functools.partial(_out_proj_kernel, bm=bm),
        grid=(t // bm,),
        in_specs=[row(yp.shape[1]), row(ya.shape[1]), _resident(w.shape), row(d),
                  _layer_resident(g, layer), c_in],
        out_specs=[row(d), row(d), c_out],
        out_shape=[jax.ShapeDtypeStruct((t, d), F32), jax.ShapeDtypeStruct((t, d), BF16), c_shape],
        compiler_params=_params(1, vmem + 4 * bm * d * 4 + c_bytes + (4 << 20)),
        name="out_proj",
    )(yp, ya, w, h, g, w_cast)


def _shift_rows(cur, prev_tail, k):
    rolled = pltpu.roll(cur, k, 0)
    first = lax.broadcasted_iota(jnp.int32, prev_tail.shape, 0) < k
    head = jnp.where(first, pltpu.roll(prev_tail, k, 0), rolled[:SUBLANES, :])
    return jnp.concatenate([head, rolled[SUBLANES:, :]], axis=0)


def _ffn_up_kernel(xn_ref, wg_ref, wu_ref, cw_ref, cb_ref, csrc_ref, o_ref, cdst_ref,
                   wgb_ref, wub_ref, carry_ref, *, bm, tiles_per_seq):
    cdst_ref[...] = csrc_ref[...].astype(BF16)
    i = pl.program_id(1)

    @pl.when(i == 0)
    def _():
        wgb_ref[...] = wg_ref[...].astype(BF16)
        wub_ref[...] = wu_ref[...].astype(BF16)

    cw = cw_ref[...]
    cb = cb_ref[...]
    @pl.when(i % tiles_per_seq == 0)
    def _():
        carry_ref[...] = jnp.zeros(carry_ref.shape, F32)

    tail = carry_ref[...]
    for r0 in range(0, bm, FFN_UP_SUB):
        rs = slice(r0, r0 + FFN_UP_SUB)
        x = xn_ref[rs, :]
        gate = jnp.dot(x, wgb_ref[...], preferred_element_type=F32)
        up = jnp.dot(x, wub_ref[...], preferred_element_type=F32)
        g1 = _shift_rows(gate, tail, 1)
        g2 = _shift_rows(gate, tail, 2)
        c = cw[0:1, :] * g2 + cw[1:2, :] * g1 + cw[2:3, :] * gate + cb
        o_ref[rs, :] = (c * _sigmoid(c) * up).astype(BF16)
        tail = gate[FFN_UP_SUB - SUBLANES:, :]
    carry_ref[...] = tail


def _ffn_up(xn, w_up, layer, conv_w, conv_b, w_cast, *, seq):
    t, d = xn.shape
    d_ff = conv_w.shape[-1]
    bm, bn = FFN_UP_BM, FFN_UP_BN
    nj = d_ff // bn
    kern = functools.partial(_ffn_up_kernel, bm=bm, tiles_per_seq=seq // bm)
    ni = t // bm
    c_in, c_out, c_shape, c_bytes = _cast_stream(w_cast, layer, nj * ni, lambda j, i: j * ni + i)
    vmem = (2 * bm * d * 2 + 2 * 2 * d * bn * 4 + 2 * d * bn * 2 + 2 * bm * bn * 2
            + 10 * FFN_UP_SUB * bn * 4 + c_bytes)
    return pl.pallas_call(
        kern,
        grid=(nj, ni),
        in_specs=[pl.BlockSpec((bm, d), lambda j, i: (i, 0)),
                  pl.BlockSpec((None, d, bn), lambda j, i: (layer, 0, j)),
                  pl.BlockSpec((None, d, bn), lambda j, i: (layer, 0, j + nj)),
                  pl.BlockSpec((None, CONV_W, bn), lambda j, i: (layer, 0, j)),
                  pl.BlockSpec((None, 1, bn), lambda j, i: (layer, 0, j)), c_in],
        out_specs=[pl.BlockSpec((bm, bn), lambda j, i: (i, j)), c_out],
        out_shape=[jax.ShapeDtypeStruct((t, d_ff), BF16), c_shape],
        scratch_shapes=[pltpu.VMEM((d, bn), BF16), pltpu.VMEM((d, bn), BF16),
                        pltpu.VMEM((SUBLANES, bn), F32)],
        compiler_params=_params(2, vmem + (4 << 20)),
        name="ffn_up",
    )(xn, w_up, w_up, conv_w, conv_b, w_cast)


def _ffn_down_kernel(a_ref, w_ref, h_ref, o_ref):
    o_ref[...] = h_ref[...] + jnp.dot(a_ref[...], w_ref[...], preferred_element_type=F32)


def _ffn_down(act, w, h):
    t, d = h.shape
    d_ff = act.shape[1]
    bm = FFN_DOWN_BM
    vmem = w.size * 2 + 2 * bm * d_ff * 2 + 4 * bm * d * 4 + 2 * bm * d * 4
    return pl.pallas_call(
        _ffn_down_kernel,
        grid=(t // bm,),
        in_specs=[pl.BlockSpec((bm, d_ff), lambda i: (i, 0)), _resident(w.shape),
                  pl.BlockSpec((bm, d), lambda i: (i, 0))],
        out_specs=pl.BlockSpec((bm, d), lambda i: (i, 0)),
        out_shape=jax.ShapeDtypeStruct((t, d), F32),
        compiler_params=_params(1, vmem + (4 << 20)),
        name="ffn_down",
    )(act, w, h)


def _ple_kernel(h_ref, g_ref, wgate_ref, p_ref, wple_ref, gfin_ref, o_ref, *, bm, final):
    for r0 in range(0, bm, ROW_SUB):
        rs = slice(r0, r0 + ROW_SUB)
        h = h_ref[rs, :]
        xn = _rms(h, g_ref[...]).astype(BF16)
        gate = _sigmoid(jnp.dot(xn, wgate_ref[...], preferred_element_type=F32))
        emb = jnp.dot(p_ref[rs, :].astype(BF16), wple_ref[...], preferred_element_type=F32)
        out = h + emb * gate
        if final:
            out = _rms(out, gfin_ref[...])
        o_ref[rs, :] = out


def _ple(h, layer, g, w_gate, p, w_ple, g_final, *, final):
    t, d = h.shape
    bm = PLE_BM
    row = lambda n: pl.BlockSpec((bm, n), lambda i: (i, 0))
    ple_dim = p.shape[-1]
    vmem = (w_gate.size * 2 + _slab_bytes(w_ple) + 4 * bm * d * 4 + 2 * bm * ple_dim * 4
            + 5 * bm * d * 4)
    return pl.pallas_call(
        functools.partial(_ple_kernel, bm=bm, final=final),
        grid=(t // bm,),
        in_specs=[row(d), _layer_resident(g, layer), _resident(w_gate.shape),
                  pl.BlockSpec((None, bm, ple_dim), lambda i: (layer, i, 0)),
                  _layer_resident(w_ple, layer), _resident(g_final.shape)],
        out_specs=row(d),
        out_shape=jax.ShapeDtypeStruct((t, d), F32),
        compiler_params=_params(1, vmem + (4 << 20)),
        name="ple",
    )(h, g, w_gate, p, w_ple, g_final)


def _rope_cols(w):
    half = ROPE_DIM // 2
    x1, x2 = w[..., :half], w[..., half:]
    pad = jnp.zeros(w.shape[:-1] + (LANES - ROPE_DIM,), w.dtype)
    return jnp.concatenate([x1, x2, pad], axis=-1), jnp.concatenate([-x2, x1, pad], axis=-1)


def _prep_w_in(w_in, o3):
    plain, rot = _rope_cols(w_in[..., o3:].astype(BF16))
    return jnp.concatenate([w_in[..., :o3].astype(BF16), plain, rot], axis=-1)


def _prep_w_uq(w_uq):
    w = w_uq.astype(BF16)
    hd = NOPE_DIM + ROPE_DIM
    head = lambda h, a, b: w[..., h * hd + a:h * hd + b]
    nope = [head(h, 0, NOPE_DIM) for h in range(N_HEADS)]
    ropes = [_rope_cols(head(h, NOPE_DIM, hd)) for h in range(N_HEADS)]
    return jnp.concatenate(nope + [r[0] for r in ropes] + [r[1] for r in ropes], axis=-1)


def _prep_w_ukv(w_ukv):
    w = w_ukv.astype(BF16)
    hd = NOPE_DIM + V_DIM
    k_nope = [w[..., h * hd:h * hd + NOPE_DIM] for h in range(N_HEADS)]
    v = [w[..., h * hd + NOPE_DIM:(h + 1) * hd] for h in range(N_HEADS)]
    return jnp.concatenate(k_nope + v, axis=-1)


def kernel(x, p, positions, norm_mix_g, w_in, pool_w, pool_scale, q_norm_g, w_uq, kv_norm_g, w_ukv,
           w_out, norm_ffn_g, w_up, conv_w, conv_b, w_down, norm_ple_g, w_ple, w_ple_gate, final_norm_g):
    batch, seq, d = x.shape
    depth = w_in.shape[0]
    t = batch * seq
    d_pool = pool_scale.shape[1]
    q_lora = q_norm_g.shape[1]
    kv_lora = kv_norm_g.shape[1]
    o3 = d_pool + q_lora + kv_lora

    inv_freq = 1.0 / (ROPE_THETA ** (jnp.arange(0, ROPE_DIM, 2, dtype=F32) / ROPE_DIM))
    ang = positions.astype(F32)[..., None] * inv_freq
    zpad = jnp.zeros((batch, seq, LANES - ROPE_DIM), F32)
    cos128 = jnp.concatenate([jnp.cos(ang), jnp.cos(ang), zpad], axis=-1).reshape(t, LANES)
    sin128 = jnp.concatenate([jnp.sin(ang), jnp.sin(ang), zpad], axis=-1).reshape(t, LANES)

    rows = lambda a: a.reshape(depth, 1, -1)
    win_b, wq_b, wkv_b = _prep_w_in(w_in, o3), _prep_w_uq(w_uq), _prep_w_ukv(w_ukv)
    pw_b, wple_b = pool_w.astype(BF16), w_ple.astype(BF16)
    g_mix, g_q, g_kv, g_ffn, g_ple = (rows(a) for a in (norm_mix_g, q_norm_g, kv_norm_g,
                                                        norm_ffn_g, norm_ple_g))
    ps, cb = rows(pool_scale), rows(conv_b)
    p3 = p.reshape(depth, t, p.shape[-1])

    h = x.reshape(t, d)
    for i in range(depth):
        ypool, qn, qr, kn, v, kr, wout_b = _front(
            h, i, g_mix, win_b, cos128, sin128, pw_b, ps, g_q, wq_b, g_kv, wkv_b, w_out,
            seq=seq, d_pool=d_pool, q_lora=q_lora, kv_lora=kv_lora)
        yatt = _attention(qn, qr, kn, kr, v, batch=batch, seq=seq)
        h, xn, wgate_b = _out_proj(ypool, yatt, wout_b, h, i, g_ffn, w_ple_gate)
        act, wdown_b = _ffn_up(xn, w_up, i, conv_w, cb, w_down, seq=seq)
        h = _ffn_down(act, wdown_b, h)
        h = _ple(h, i, g_ple, wgate_b, p3, wple_b, final_norm_g.reshape(1, -1),
                 final=(i == depth - 1))
    return h.reshape(batch, seq, d)
```

```python
import functools
import math

import jax
import jax.numpy as jnp
from jax import lax
from jax.experimental import pallas as pl
from jax.experimental.pallas import tpu as pltpu

F32 = jnp.float32
BF16 = jnp.bfloat16

POOL_WINDOWS = (2, 4, 8, 16)
N_POOL_GROUPS = len(POOL_WINDOWS)
N_HEADS = 8
NOPE_DIM = 128
ROPE_DIM = 64
V_DIM = 128
ROPE_THETA = 10000.0
ATTN_C2 = math.log2(math.e) / math.sqrt(NOPE_DIM + ROPE_DIM)
CONV_W = 3
EPS = 1e-6

LANES = 128
SUBLANES = 8
V7X_SCOPED_VMEM_BYTES = 60000 * 1024

POOL_HALO = 16

FRONT_BM = 512
ATTN_BK = 256
ATTN_HEADS = 2
OUT_BM = 512
FFN_UP_BM = 1024
FFN_UP_SUB = 256
FFN_UP_BN = 512
FFN_DOWN_BM = 256
PLE_BM = 512
ROW_SUB = 256


def _rms(x, g):
    ms = jnp.mean(x * x, axis=-1, keepdims=True)
    return (x * lax.rsqrt(ms + EPS)) * g


def _sigmoid(x):
    return 1.0 / (1.0 + jnp.exp(-x))


def _resident(shape):
    nd = len(shape)
    return pl.BlockSpec(shape, lambda *_: (0,) * nd, pipeline_mode=pl.Buffered(1))


def _layer_resident(stacked, layer):
    shape = stacked.shape[1:]
    return pl.BlockSpec((None,) + shape, lambda *_: (layer,) + (0,) * len(shape),
                        pipeline_mode=pl.Buffered(1))


def _cast_stream(stacked, layer, n_steps, step_of):
    rows, cols = stacked.shape[1:]
    chunk = rows // n_steps
    assert chunk * n_steps == rows and chunk % (2 * SUBLANES) == 0, (rows, n_steps)
    in_spec = pl.BlockSpec((None, chunk, cols), lambda *idx: (layer, step_of(*idx), 0))
    out_spec = pl.BlockSpec((chunk, cols), lambda *idx: (step_of(*idx), 0))
    return in_spec, out_spec, jax.ShapeDtypeStruct((rows, cols), BF16), chunk * cols * (4 + 2) * 2


def _slab_bytes(stacked):
    return math.prod(stacked.shape[1:]) * stacked.dtype.itemsize


def _params(n_axes, vmem_bytes):
    return pltpu.CompilerParams(
        dimension_semantics=("arbitrary",) * n_axes,
        vmem_limit_bytes=min(int(vmem_bytes), V7X_SCOPED_VMEM_BYTES),
    )


def _front_kernel(h_ref, g_ref, win_ref, cos_ref, sin_ref, pw_ref, ps_ref,
                  gq_ref, wq_ref, gkv_ref, wkv_ref, csrc_ref,
                  ypool_ref, qn_ref, qr_ref, kn_ref, v_ref, kr_ref, cdst_ref,
                  carry_ref, *, bm, tiles_per_seq, d_pool, q_lora, kv_lora):
    cdst_ref[...] = csrc_ref[...].astype(BF16)
    i = pl.program_id(0)
    seq_tile = i % tiles_per_seq
    pool_c = d_pool // N_POOL_GROUPS
    o1 = d_pool
    o2 = o1 + q_lora
    o3 = o2 + kv_lora
    hn = N_HEADS * NOPE_DIM
    hr = N_HEADS * LANES

    @pl.when(seq_tile == 0)
    def _():
        carry_ref[...] = jnp.zeros(carry_ref.shape, F32)

    tail = carry_ref[...]
    for r0 in range(0, bm, ROW_SUB):
        rs = slice(r0, r0 + ROW_SUB)
        xn = _rms(h_ref[rs, :], g_ref[...]).astype(BF16)
        u = jnp.dot(xn, win_ref[...], preferred_element_type=F32)

        cos = cos_ref[rs, :]
        sin = sin_ref[rs, :]
        kr_ref[rs, :] = (u[:, o3:o3 + LANES] * cos + u[:, o3 + LANES:o3 + 2 * LANES] * sin).astype(BF16)

        cqn = _rms(u[:, o1:o2], gq_ref[...]).astype(BF16)
        q = jnp.dot(cqn, wq_ref[...], preferred_element_type=F32)
        qn_ref[rs, :] = (q[:, :hn] * ATTN_C2).astype(BF16)
        cos_h = jnp.concatenate([cos] * N_HEADS, axis=1)
        sin_h = jnp.concatenate([sin] * N_HEADS, axis=1)
        q_rope = q[:, hn:hn + hr] * cos_h + q[:, hn + hr:hn + 2 * hr] * sin_h
        qr_ref[rs, :] = (q_rope * ATTN_C2).astype(BF16)

        ckvn = _rms(u[:, o2:o3], gkv_ref[...]).astype(BF16)
        kv = jnp.dot(ckvn, wkv_ref[...], preferred_element_type=F32)
        kn_ref[rs, :] = kv[:, :hn].astype(BF16)
        v_ref[rs, :] = kv[:, hn:].astype(BF16)

        up = u[:, :o1]
        t1 = lax.broadcasted_iota(jnp.int32, (ROW_SUB, LANES), 0) + (seq_tile * bm + r0 + 1)
        for grp, w in enumerate(POOL_WINDOWS):
            c0, c1 = grp * pool_c, (grp + 1) * pool_c
            ext = jnp.concatenate([tail[:, c0:c1], up[:, c0:c1]], axis=0)
            acc = ext + pltpu.roll(ext, 1, 0)
            span = 2
            while span < w:
                acc = acc + pltpu.roll(acc, span, 0)
                span *= 2
            inv = 1.0 / jnp.minimum(t1, w).astype(F32)
            inv = jnp.concatenate([inv] * (pool_c // LANES), axis=1)
            diff = (acc[POOL_HALO:, :] * inv - up[:, c0:c1]).astype(BF16)
            y = jnp.dot(diff, pw_ref[grp], preferred_element_type=F32) * ps_ref[:, c0:c1]
            ypool_ref[rs, c0:c1] = y.astype(BF16)
        tail = up[ROW_SUB - POOL_HALO:, :]
    carry_ref[...] = tail


def _front(h, layer, g, win, cos128, sin128, pw, ps, gq, wq, gkv, wkv, w_cast, *,
           seq, d_pool, q_lora, kv_lora):
    t, d = h.shape
    bm = FRONT_BM
    hn = N_HEADS * NOPE_DIM
    row = lambda n: pl.BlockSpec((bm, n), lambda i: (i, 0))
    res = lambda a: _layer_resident(a, layer)
    kern = functools.partial(_front_kernel, bm=bm, tiles_per_seq=seq // bm,
                             d_pool=d_pool, q_lora=q_lora, kv_lora=kv_lora)
    weights = win.size * 2 + _slab_bytes(wq) + _slab_bytes(wkv) + _slab_bytes(pw)
    tiles = 2 * bm * (d * 4 + 2 * LANES * 4 + (d_pool + 4 * hn + LANES) * 2)
    temps = bm * (win.shape[-1] + wq.shape[-1] + wkv.shape[-1] + 4 * d_pool) * 4 + bm * d * 6
    c_in, c_out, c_shape, c_bytes = _cast_stream(w_cast, layer, t // bm, lambda i: i)
    return pl.pallas_call(
        kern,
        grid=(t // bm,),
        in_specs=[row(d), res(g), _resident(win.shape), row(LANES), row(LANES),
                  res(pw), res(ps), res(gq), res(wq), res(gkv), res(wkv), c_in],
        out_specs=[row(d_pool), row(hn), row(hn), row(hn), row(hn), row(LANES), c_out],
        out_shape=[jax.ShapeDtypeStruct((t, d_pool), BF16)] + [jax.ShapeDtypeStruct((t, hn), BF16)] * 4
                  + [jax.ShapeDtypeStruct((t, LANES), BF16), c_shape],
        scratch_shapes=[pltpu.VMEM((POOL_HALO, d_pool), F32)],
        compiler_params=_params(1, weights + tiles + temps + c_bytes),
        name="front",
    )(h, g, win, cos128, sin128, pw, ps, gq, wq, gkv, wkv, w_cast)


def _attn_kernel(qn_ref, qr_ref, kn_ref, kr_ref, v_ref, o_ref,
                 vt_ref, t_ref, p_ref, acc_ref, *, seq, bk, heads):
    bq = 2 * bk
    nt_dims = (((1,), (1,)), ((), ()))

    def krows(c):
        return pl.ds(pl.multiple_of(c * bk, bk), bk)

    def qrows(c):
        return pl.ds(pl.multiple_of(c * bq, bq), bq)

    ones_rows = (lax.broadcasted_iota(jnp.int32, (SUBLANES, bk), 0) == 0).astype(F32)

    def load_vt(c, _):
        for hh in range(heads):
            v = v_ref[krows(c), hh * V_DIM:(hh + 1) * V_DIM]
            vt_ref[hh, c] = jnp.concatenate([v.astype(F32).T, ones_rows], axis=0).astype(BF16)
        return 0

    lax.fori_loop(0, seq // bk, load_vt, 0)

    def scores(q, kj, slot):
        kr = kr_ref[krows(kj), :]
        for hh in range(heads):
            k = jnp.concatenate([kn_ref[krows(kj), hh * NOPE_DIM:(hh + 1) * NOPE_DIM], kr], axis=-1)
            t_ref[slot, hh] = lax.dot_general(k, q[hh], nt_dims, preferred_element_type=F32)

    def softmax(slot, ms, diag_offset=None):
        new_ms, alphas = [], []
        for hh in range(heads):
            m = ms[hh]
            t = t_ref[slot, hh]
            if diag_offset is not None:
                key = lax.broadcasted_iota(jnp.int32, (bk, bq), 0) + diag_offset
                qry = lax.broadcasted_iota(jnp.int32, (bk, bq), 1)
                t = jnp.where(key <= qry, t, -jnp.inf)
            m_new = jnp.maximum(m, jnp.max(t, axis=0, keepdims=True))
            p_ref[slot, hh] = jnp.exp2(t - m_new).astype(BF16)
            new_ms.append(m_new)
            alphas.append(jnp.exp2(m - m_new))
        return tuple(new_ms), tuple(alphas)

    def values(kj, slot, alphas):
        for hh in range(heads):
            acc_ref[hh] = alphas[hh] * acc_ref[hh] + jnp.dot(
                vt_ref[hh, kj], p_ref[slot, hh], preferred_element_type=F32)

    nq = seq // bq

    def load_q(qi):
        return [jnp.concatenate([qn_ref[qrows(qi), hh * NOPE_DIM:(hh + 1) * NOPE_DIM],
                                 qr_ref[qrows(qi), hh * LANES:(hh + 1) * LANES]], axis=-1)
                for hh in range(heads)]

    scores(load_q(0), 0, 0)

    def q_body(qi, _):
        q = load_q(qi)
        acc_ref[...] = jnp.zeros(acc_ref.shape, F32)
        p_ref[1] = jnp.zeros(p_ref.shape[1:], BF16)
        ms = tuple(jnp.full((1, bq), -jnp.inf, F32) for _ in range(heads))
        alphas = tuple(jnp.ones((1, bq), F32) for _ in range(heads))

        def pair(b, carry):
            ms, alphas = carry
            values(jnp.maximum(b - 1, 0), 1, alphas)
            ms, alphas = softmax(0, ms)
            scores(q, b + 1, 1)
            values(b, 0, alphas)
            ms, alphas = softmax(1, ms)
            scores(q, b + 2, 0)
            return ms, alphas

        carry = lax.fori_loop(0, qi // 2, lambda u, c: pair(4 * u + 2, pair(4 * u, c)), (ms, alphas))
        ms, alphas = lax.fori_loop(qi - qi % 2, qi, lambda u, c: pair(2 * u, c), carry)
        b = 2 * qi
        values(jnp.maximum(b - 1, 0), 1, alphas)
        ms, alphas = softmax(0, ms, diag_offset=0)
        scores(q, b + 1, 1)
        scores(load_q(jnp.minimum(qi + 1, nq - 1)), 0, 0)
        values(b, 0, alphas)
        ms, alphas = softmax(1, ms, diag_offset=bk)
        values(b + 1, 1, alphas)
        for hh in range(heads):
            out = acc_ref[hh, :V_DIM, :] / acc_ref[hh, V_DIM:V_DIM + 1, :]
            o_ref[qrows(qi), hh * V_DIM:(hh + 1) * V_DIM] = out.T.astype(BF16)
        return 0

    lax.fori_loop(0, nq, q_body, 0)


def _attention(qn, qr, kn, kr, v, *, batch, seq):
    bk, heads = ATTN_BK, ATTN_HEADS
    bq = 2 * bk
    hn = N_HEADS * NOPE_DIM
    w = heads * LANES
    r3 = lambda a: a.reshape(batch, seq, a.shape[-1])
    hspec = pl.BlockSpec((None, seq, w), lambda b, hp: (b, 0, hp))
    krspec = pl.BlockSpec((None, seq, LANES), lambda b, hp: (b, 0, 0))
    kern = functools.partial(_attn_kernel, seq=seq, bk=bk, heads=heads)
    vext = V_DIM + SUBLANES
    scratch = heads * seq * vext * 2 + 2 * heads * bk * bq * (4 + 2) + heads * vext * bq * 4
    vmem = 2 * (5 * seq * w + seq * LANES) * 2 + scratch + heads * 8 * bk * bq * 4
    out = pl.pallas_call(
        kern,
        grid=(batch, N_HEADS // heads),
        in_specs=[hspec, hspec, hspec, krspec, hspec],
        out_specs=hspec,
        out_shape=jax.ShapeDtypeStruct((batch, seq, hn), BF16),
        scratch_shapes=[pltpu.VMEM((heads, seq // bk, vext, bk), BF16),
                        pltpu.VMEM((2, heads, bk, bq), F32),
                        pltpu.VMEM((2, heads, bk, bq), BF16),
                        pltpu.VMEM((heads, vext, bq), F32)],
        compiler_params=_params(2, vmem),
        name="attention",
    )(r3(qn), r3(qr), r3(kn), r3(kr), r3(v))
    return out.reshape(batch * seq, hn)


def _out_proj_kernel(yp_ref, ya_ref, w_ref, h_ref, g_ref, csrc_ref, o_ref, xn_ref, cdst_ref, *, bm):
    cdst_ref[...] = csrc_ref[...].astype(BF16)
    for r0 in range(0, bm, ROW_SUB):
        rs = slice(r0, r0 + ROW_SUB)
        y = jnp.concatenate([yp_ref[rs, :], ya_ref[rs, :]], axis=-1)
        hn = h_ref[rs, :] + jnp.dot(y, w_ref[...], preferred_element_type=F32)
        o_ref[rs, :] = hn
        xn_ref[rs, :] = _rms(hn, g_ref[...]).astype(BF16)


def _out_proj(yp, ya, w, h, layer, g, w_cast):
    t, d = h.shape
    bm = OUT_BM
    row = lambda n: pl.BlockSpec((bm, n), lambda i: (i, 0))
    c_in, c_out, c_shape, c_bytes = _cast_stream(w_cast, layer, t // bm, lambda i: i)
    vmem = w.size * 2 + 2 * bm * (yp.shape[1] + ya.shape[1]) * 2 + 4 * bm * d * 4 + 2 * bm * d * 2
    return pl.pallas_call(
        functools.partial(_out_proj_kernel, bm=bm),
        grid=(t // bm,),
        in_specs=[row(yp.shape[1]), row(ya.shape[1]), _resident(w.shape), row(d),
                  _layer_resident(g, layer), c_in],
        out_specs=[row(d), row(d), c_out],
        out_shape=[jax.ShapeDtypeStruct((t, d), F32), jax.ShapeDtypeStruct((t, d), BF16), c_shape],
        compiler_params=_params(1, vmem + 4 * bm * d * 4 + c_bytes + (4 << 20)),
        name="out_proj",
    )(yp, ya, w, h, g, w_cast)


def _shift_rows(cur, prev_tail, k):
    rolled = pltpu.roll(cur, k, 0)
    first = lax.broadcasted_iota(jnp.int32, prev_tail.shape, 0) < k
    head = jnp.where(first, pltpu.roll(prev_tail, k, 0), rolled[:SUBLANES, :])
    return jnp.concatenate([head, rolled[SUBLANES:, :]], axis=0)


def _ffn_up_kernel(xn_ref, wg_ref, wu_ref, cw_ref, cb_ref, csrc_ref, o_ref, cdst_ref,
                   wgb_ref, wub_ref, carry_ref, *, bm, tiles_per_seq):
    cdst_ref[...] = csrc_ref[...].astype(BF16)
    i = pl.program_id(1)

    @pl.when(i == 0)
    def _():
        wgb_ref[...] = wg_ref[...].astype(BF16)
        wub_ref[...] = wu_ref[...].astype(BF16)

    cw = cw_ref[...]
    cb = cb_ref[...]
    @pl.when(i % tiles_per_seq == 0)
    def _():
        carry_ref[...] = jnp.zeros(carry_ref.shape, F32)

    tail = carry_ref[...]
    for r0 in range(0, bm, FFN_UP_SUB):
        rs = slice(r0, r0 + FFN_UP_SUB)
        x = xn_ref[rs, :]
        gate = jnp.dot(x, wgb_ref[...], preferred_element_type=F32)
        up = jnp.dot(x, wub_ref[...], preferred_element_type=F32)
        g1 = _shift_rows(gate, tail, 1)
        g2 = _shift_rows(gate, tail, 2)
        c = cw[0:1, :] * g2 + cw[1:2, :] * g1 + cw[2:3, :] * gate + cb
        o_ref[rs, :] = (c * _sigmoid(c) * up).astype(BF16)
        tail = gate[FFN_UP_SUB - SUBLANES:, :]
    carry_ref[...] = tail


def _ffn_up(xn, w_up, layer, conv_w, conv_b, w_cast, *, seq):
    t, d = xn.shape
    d_ff = conv_w.shape[-1]
    bm, bn = FFN_UP_BM, FFN_UP_BN
    nj = d_ff // bn
    kern = functools.partial(_ffn_up_kernel, bm=bm, tiles_per_seq=seq // bm)
    ni = t // bm
    c_in, c_out, c_shape, c_bytes = _cast_stream(w_cast, layer, nj * ni, lambda j, i: j * ni + i)
    vmem = (2 * bm * d * 2 + 2 * 2 * d * bn * 4 + 2 * d * bn * 2 + 2 * bm * bn * 2
            + 10 * FFN_UP_SUB * bn * 4 + c_bytes)
    return pl.pallas_call(
        kern,
        grid=(nj, ni),
        in_specs=[pl.BlockSpec((bm, d), lambda j, i: (i, 0)),
                  pl.BlockSpec((None, d, bn), lambda j, i: (layer, 0, j)),
                  pl.BlockSpec((None, d, bn), lambda j, i: (layer, 0, j + nj)),
                  pl.BlockSpec((None, CONV_W, bn), lambda j, i: (layer, 0, j)),
                  pl.BlockSpec((None, 1, bn), lambda j, i: (layer, 0, j)), c_in],
        out_specs=[pl.BlockSpec((bm, bn), lambda j, i: (i, j)), c_out],
        out_shape=[jax.ShapeDtypeStruct((t, d_ff), BF16), c_shape],
        scratch_shapes=[pltpu.VMEM((d, bn), BF16), pltpu.VMEM((d, bn), BF16),
                        pltpu.VMEM((SUBLANES, bn), F32)],
        compiler_params=_params(2, vmem + (4 << 20)),
        name="ffn_up",
    )(xn, w_up, w_up, conv_w, conv_b, w_cast)


def _ffn_down_kernel(a_ref, w_ref, h_ref, o_ref):
    o_ref[...] = h_ref[...] + jnp.dot(a_ref[...], w_ref[...], preferred_element_type=F32)


def _ffn_down(act, w, h):
    t, d = h.shape
    d_ff = act.shape[1]
    bm = FFN_DOWN_BM
    vmem = w.size * 2 + 2 * bm * d_ff * 2 + 4 * bm * d * 4 + 2 * bm * d * 4
    return pl.pallas_call(
        _ffn_down_kernel,
        grid=(t // bm,),
        in_specs=[pl.BlockSpec((bm, d_ff), lambda i: (i, 0)), _resident(w.shape),
                  pl.BlockSpec((bm, d), lambda i: (i, 0))],
        out_specs=pl.BlockSpec((bm, d), lambda i: (i, 0)),
        out_shape=jax.ShapeDtypeStruct((t, d), F32),
        compiler_params=_params(1, vmem + (4 << 20)),
        name="ffn_down",
    )(act, w, h)


def _ple_kernel(*refs, bm, final, n_main):
    if n_main:
        (h_ref, g_ref, wgate_ref, p_ref, wple_ref, gfin_ref, csrc_ref, crope_ref,
         o_ref, cdst_ref) = refs
        cdst_ref[:, :n_main] = csrc_ref[...].astype(BF16)
        cdst_ref[:, n_main:] = crope_ref[...]
    else:
        h_ref, g_ref, wgate_ref, p_ref, wple_ref, gfin_ref, o_ref = refs
    for r0 in range(0, bm, ROW_SUB):
        rs = slice(r0, r0 + ROW_SUB)
        h = h_ref[rs, :]
        xn = _rms(h, g_ref[...]).astype(BF16)
        gate = _sigmoid(jnp.dot(xn, wgate_ref[...], preferred_element_type=F32))
        emb = jnp.dot(p_ref[rs, :].astype(BF16), wple_ref[...], preferred_element_type=F32)
        out = h + emb * gate
        if final:
            out = _rms(out, gfin_ref[...])
        o_ref[rs, :] = out


def _ple(h, layer, g, w_gate, p, w_ple, g_final, w_in=None, w_rope=None, *, final):
    t, d = h.shape
    bm = PLE_BM
    row = lambda n: pl.BlockSpec((bm, n), lambda i: (i, 0))
    ple_dim = p.shape[-1]
    vmem = (w_gate.size * 2 + _slab_bytes(w_ple) + 4 * bm * d * 4 + 2 * bm * ple_dim * 4
            + 5 * bm * d * 4)
    in_specs = [row(d), _layer_resident(g, layer), _resident(w_gate.shape),
                pl.BlockSpec((None, bm, ple_dim), lambda i: (layer, i, 0)),
                _layer_resident(w_ple, layer), _resident(g_final.shape)]
    out_specs, out_shape = [row(d)], [jax.ShapeDtypeStruct((t, d), F32)]
    args = [h, g, w_gate, p, w_ple, g_final]
    n_main = 0
    if w_in is not None:
        rows, n_rope = w_rope.shape[1:]
        n_main = w_in.shape[2] - ROPE_DIM
        chunk = rows // (t // bm)
        assert chunk * (t // bm) == rows and chunk % (2 * SUBLANES) == 0 and n_main % LANES == 0
        nxt = layer + 1
        in_specs += [pl.BlockSpec((None, chunk, n_main), lambda i: (nxt, i, 0)),
                     pl.BlockSpec((None, chunk, n_rope), lambda i: (nxt, i, 0))]
        out_specs.append(pl.BlockSpec((chunk, n_main + n_rope), lambda i: (i, 0)))
        out_shape.append(jax.ShapeDtypeStruct((rows, n_main + n_rope), BF16))
        args += [w_in, w_rope]
        vmem += 2 * chunk * (n_main * 4 + n_rope * 2 + (n_main + n_rope) * 2)
    outs = pl.pallas_call(
        functools.partial(_ple_kernel, bm=bm, final=final, n_main=n_main),
        grid=(t // bm,),
        in_specs=in_specs,
        out_specs=out_specs,
        out_shape=out_shape,
        compiler_params=_params(1, vmem + (4 << 20)),
        name="ple",
    )(*args)
    return outs if w_in is not None else outs[0]


def _rope_cols(w):
    half = ROPE_DIM // 2
    x1, x2 = w[..., :half], w[..., half:]
    pad = jnp.zeros(w.shape[:-1] + (LANES - ROPE_DIM,), w.dtype)
    return jnp.concatenate([x1, x2, pad], axis=-1), jnp.concatenate([-x2, x1, pad], axis=-1)


def _prep_w_in_rope(w_in, o3):
    return jnp.concatenate(_rope_cols(w_in[..., o3:].astype(BF16)), axis=-1)


def _prep_w_uq(w_uq):
    w = w_uq.astype(BF16)
    hd = NOPE_DIM + ROPE_DIM
    head = lambda h, a, b: w[..., h * hd + a:h * hd + b]
    nope = [head(h, 0, NOPE_DIM) for h in range(N_HEADS)]
    ropes = [_rope_cols(head(h, NOPE_DIM, hd)) for h in range(N_HEADS)]
    return jnp.concatenate(nope + [r[0] for r in ropes] + [r[1] for r in ropes], axis=-1)


def _prep_w_ukv(w_ukv):
    w = w_ukv.astype(BF16)
    hd = NOPE_DIM + V_DIM
    k_nope = [w[..., h * hd:h * hd + NOPE_DIM] for h in range(N_HEADS)]
    v = [w[..., h * hd + NOPE_DIM:(h + 1) * hd] for h in range(N_HEADS)]
    return jnp.concatenate(k_nope + v, axis=-1)


def kernel(x, p, positions, norm_mix_g, w_in, pool_w, pool_scale, q_norm_g, w_uq, kv_norm_g, w_ukv,
           w_out, norm_ffn_g, w_up, conv_w, conv_b, w_down, norm_ple_g, w_ple, w_ple_gate, final_norm_g):
    batch, seq, d = x.shape
    depth = w_in.shape[0]
    t = batch * seq
    d_pool = pool_scale.shape[1]
    q_lora = q_norm_g.shape[1]
    kv_lora = kv_norm_g.shape[1]
    o3 = d_pool + q_lora + kv_lora

    inv_freq = 1.0 / (ROPE_THETA ** (jnp.arange(0, ROPE_DIM, 2, dtype=F32) / ROPE_DIM))
    ang = positions.astype(F32)[..., None] * inv_freq
    zpad = jnp.zeros((batch, seq, LANES - ROPE_DIM), F32)
    cos128 = jnp.concatenate([jnp.cos(ang), jnp.cos(ang), zpad], axis=-1).reshape(t, LANES)
    sin128 = jnp.concatenate([jnp.sin(ang), jnp.sin(ang), zpad], axis=-1).reshape(t, LANES)

    rows = lambda a: a.reshape(depth, 1, -1)
    wrope_b, wq_b, wkv_b = _prep_w_in_rope(w_in, o3), _prep_w_uq(w_uq), _prep_w_ukv(w_ukv)
    win_b = jnp.concatenate([w_in[0, :, :o3].astype(BF16), wrope_b[0]], axis=-1)
    pw_b, wple_b = pool_w.astype(BF16), w_ple.astype(BF16)
    g_mix, g_q, g_kv, g_ffn, g_ple = (rows(a) for a in (norm_mix_g, q_norm_g, kv_norm_g,
                                                        norm_ffn_g, norm_ple_g))
    ps, cb = rows(pool_scale), rows(conv_b)
    p3 = p.reshape(depth, t, p.shape[-1])

    h = x.reshape(t, d)
    for i in range(depth):
        ypool, qn, qr, kn, v, kr, wout_b = _front(
            h, i, g_mix, win_b, cos128, sin128, pw_b, ps, g_q, wq_b, g_kv, wkv_b, w_out,
            seq=seq, d_pool=d_pool, q_lora=q_lora, kv_lora=kv_lora)
        yatt = _attention(qn, qr, kn, kr, v, batch=batch, seq=seq)
        h, xn, wgate_b = _out_proj(ypool, yatt, wout_b, h, i, g_ffn, w_ple_gate)
        act, wdown_b = _ffn_up(xn, w_up, i, conv_w, cb, w_down, seq=seq)
        h = _ffn_down(act, wdown_b, h)
        if i < depth - 1:
            h, win_b = _ple(h, i, g_ple, wgate_b, p3, wple_b, final_norm_g.reshape(1, -1),
                            w_in, wrope_b, final=False)
        else:
            h = _ple(h, i, g_ple, wgate_b, p3, wple_b, final_norm_g.reshape(1, -1), final=True)
    return h.reshape(batch, seq, d)
```

```python
import functools
import math

import jax
import jax.numpy as jnp
from jax import lax
from jax.experimental import pallas as pl
from jax.experimental.pallas import tpu as pltpu

F32 = jnp.float32
BF16 = jnp.bfloat16

POOL_WINDOWS = (2, 4, 8, 16)
N_POOL_GROUPS = len(POOL_WINDOWS)
N_HEADS = 8
NOPE_DIM = 128
ROPE_DIM = 64
V_DIM = 128
ROPE_THETA = 10000.0
ATTN_C2 = math.log2(math.e) / math.sqrt(NOPE_DIM + ROPE_DIM)
CONV_W = 3
EPS = 1e-6

LANES = 128
SUBLANES = 8
V7X_SCOPED_VMEM_BYTES = 60000 * 1024

POOL_HALO = 16

FRONT_BM = 512
ATTN_BK = 256
ATTN_HEADS = 2
OUT_BM = 512
FFN_UP_BM = 1024
FFN_UP_SUB = 256
FFN_UP_BN = 512
FFN_DOWN_BM = 256
PLE_BM = 512
ROW_SUB = 256


def _rms(x, g):
    ms = jnp.mean(x * x, axis=-1, keepdims=True)
    return (x * lax.rsqrt(ms + EPS)) * g


def _sigmoid(x):
    return 1.0 / (1.0 + jnp.exp(-x))


def _resident(shape):
    nd = len(shape)
    return pl.BlockSpec(shape, lambda *_: (0,) * nd, pipeline_mode=pl.Buffered(1))


def _layer_resident(stacked, layer):
    shape = stacked.shape[1:]
    return pl.BlockSpec((None,) + shape, lambda *_: (layer,) + (0,) * len(shape),
                        pipeline_mode=pl.Buffered(1))


def _cast_stream(stacked, layer, n_steps, step_of):
    rows, cols = stacked.shape[1:]
    chunk = rows // n_steps
    assert chunk * n_steps == rows and chunk % (2 * SUBLANES) == 0, (rows, n_steps)
    in_spec = pl.BlockSpec((None, chunk, cols), lambda *idx: (layer, step_of(*idx), 0))
    out_spec = pl.BlockSpec((chunk, cols), lambda *idx: (step_of(*idx), 0))
    return in_spec, out_spec, jax.ShapeDtypeStruct((rows, cols), BF16), chunk * cols * (4 + 2) * 2


def _slab_bytes(stacked):
    return math.prod(stacked.shape[1:]) * stacked.dtype.itemsize


def _params(n_axes, vmem_bytes):
    return pltpu.CompilerParams(
        dimension_semantics=("arbitrary",) * n_axes,
        vmem_limit_bytes=min(int(vmem_bytes), V7X_SCOPED_VMEM_BYTES),
    )


def _front_kernel(h_ref, g_ref, win_ref, cos_ref, sin_ref, pw_ref, ps_ref,
                  gq_ref, wq_ref, gkv_ref, wkv_ref, csrc_ref,
                  ypool_ref, qn_ref, qr_ref, kn_ref, v_ref, kr_ref, cdst_ref,
                  carry_ref, *, bm, tiles_per_seq, d_pool, q_lora, kv_lora):
    cdst_ref[...] = csrc_ref[...].astype(BF16)
    i = pl.program_id(0)
    seq_tile = i % tiles_per_seq
    pool_c = d_pool // N_POOL_GROUPS
    o1 = d_pool
    o2 = o1 + q_lora
    o3 = o2 + kv_lora
    hn = N_HEADS * NOPE_DIM
    hr = N_HEADS * LANES

    @pl.when(seq_tile == 0)
    def _():
        carry_ref[...] = jnp.zeros(carry_ref.shape, F32)

    tail = carry_ref[...]
    for r0 in range(0, bm, ROW_SUB):
        rs = slice(r0, r0 + ROW_SUB)
        xn = _rms(h_ref[rs, :], g_ref[...]).astype(BF16)
        u = jnp.dot(xn, win_ref[...], preferred_element_type=F32)

        cos = cos_ref[rs, :]
        sin = sin_ref[rs, :]
        kr_ref[rs, :] = (u[:, o3:o3 + LANES] * cos + u[:, o3 + LANES:o3 + 2 * LANES] * sin).astype(BF16)

        cqn = _rms(u[:, o1:o2], gq_ref[...]).astype(BF16)
        q = jnp.dot(cqn, wq_ref[...], preferred_element_type=F32)
        qn_ref[rs, :] = (q[:, :hn] * ATTN_C2).astype(BF16)
        cos_h = jnp.concatenate([cos] * N_HEADS, axis=1)
        sin_h = jnp.concatenate([sin] * N_HEADS, axis=1)
        q_rope = q[:, hn:hn + hr] * cos_h + q[:, hn + hr:hn + 2 * hr] * sin_h
        qr_ref[rs, :] = (q_rope * ATTN_C2).astype(BF16)

        ckvn = _rms(u[:, o2:o3], gkv_ref[...]).astype(BF16)
        kv = jnp.dot(ckvn, wkv_ref[...], preferred_element_type=F32)
        kn_ref[rs, :] = kv[:, :hn].astype(BF16)
        v_ref[rs, :] = kv[:, hn:].astype(BF16)

        up = u[:, :o1]
        t1 = lax.broadcasted_iota(jnp.int32, (ROW_SUB, LANES), 0) + (seq_tile * bm + r0 + 1)
        for grp, w in enumerate(POOL_WINDOWS):
            c0, c1 = grp * pool_c, (grp + 1) * pool_c
            ext = jnp.concatenate([tail[:, c0:c1], up[:, c0:c1]], axis=0)
            acc = ext + pltpu.roll(ext, 1, 0)
            span = 2
            while span < w:
                acc = acc + pltpu.roll(acc, span, 0)
                span *= 2
            inv = 1.0 / jnp.minimum(t1, w).astype(F32)
            inv = jnp.concatenate([inv] * (pool_c // LANES), axis=1)
            diff = (acc[POOL_HALO:, :] * inv - up[:, c0:c1]).astype(BF16)
            y = jnp.dot(diff, pw_ref[grp], preferred_element_type=F32) * ps_ref[:, c0:c1]
            ypool_ref[rs, c0:c1] = y.astype(BF16)
        tail = up[ROW_SUB - POOL_HALO:, :]
    carry_ref[...] = tail


def _front(h, layer, g, win, cos128, sin128, pw, ps, gq, wq, gkv, wkv, w_cast, *,
           seq, d_pool, q_lora, kv_lora):
    t, d = h.shape
    bm = FRONT_BM
    hn = N_HEADS * NOPE_DIM
    row = lambda n: pl.BlockSpec((bm, n), lambda i: (i, 0))
    res = lambda a: _layer_resident(a, layer)
    kern = functools.partial(_front_kernel, bm=bm, tiles_per_seq=seq // bm,
                             d_pool=d_pool, q_lora=q_lora, kv_lora=kv_lora)
    weights = _slab_bytes(win) + _slab_bytes(wq) + _slab_bytes(wkv) + _slab_bytes(pw)
    tiles = 2 * bm * (d * 4 + 2 * LANES * 4 + (d_pool + 4 * hn + LANES) * 2)
    temps = bm * (win.shape[-1] + wq.shape[-1] + wkv.shape[-1] + 4 * d_pool) * 4 + bm * d * 6
    c_in, c_out, c_shape, c_bytes = _cast_stream(w_cast, layer, t // bm, lambda i: i)
    return pl.pallas_call(
        kern,
        grid=(t // bm,),
        in_specs=[row(d), res(g), res(win), row(LANES), row(LANES),
                  res(pw), res(ps), res(gq), res(wq), res(gkv), res(wkv), c_in],
        out_specs=[row(d_pool), row(hn), row(hn), row(hn), row(hn), row(LANES), c_out],
        out_shape=[jax.ShapeDtypeStruct((t, d_pool), BF16)] + [jax.ShapeDtypeStruct((t, hn), BF16)] * 4
                  + [jax.ShapeDtypeStruct((t, LANES), BF16), c_shape],
        scratch_shapes=[pltpu.VMEM((POOL_HALO, d_pool), F32)],
        compiler_params=_params(1, weights + tiles + temps + c_bytes),
        name="front",
    )(h, g, win, cos128, sin128, pw, ps, gq, wq, gkv, wkv, w_cast)


def _attn_kernel(qn_ref, qr_ref, kn_ref, kr_ref, v_ref, o_ref,
                 vt_ref, t_ref, p_ref, acc_ref, *, seq, bk, heads):
    bq = 2 * bk
    nt_dims = (((1,), (1,)), ((), ()))

    def krows(c):
        return pl.ds(pl.multiple_of(c * bk, bk), bk)

    def qrows(c):
        return pl.ds(pl.multiple_of(c * bq, bq), bq)

    ones_rows = (lax.broadcasted_iota(jnp.int32, (SUBLANES, bk), 0) == 0).astype(F32)

    def load_vt(c, _):
        for hh in range(heads):
            v = v_ref[krows(c), hh * V_DIM:(hh + 1) * V_DIM]
            vt_ref[hh, c] = jnp.concatenate([v.astype(F32).T, ones_rows], axis=0).astype(BF16)
        return 0

    lax.fori_loop(0, seq // bk, load_vt, 0)

    def scores(q, kj, slot):
        kr = kr_ref[krows(kj), :]
        for hh in range(heads):
            k = jnp.concatenate([kn_ref[krows(kj), hh * NOPE_DIM:(hh + 1) * NOPE_DIM], kr], axis=-1)
            t_ref[slot, hh] = lax.dot_general(k, q[hh], nt_dims, preferred_element_type=F32)

    def softmax(slot, ms, diag_offset=None):
        new_ms, alphas = [], []
        for hh in range(heads):
            m = ms[hh]
            t = t_ref[slot, hh]
            if diag_offset is not None:
                key = lax.broadcasted_iota(jnp.int32, (bk, bq), 0) + diag_offset
                qry = lax.broadcasted_iota(jnp.int32, (bk, bq), 1)
                t = jnp.where(key <= qry, t, -jnp.inf)
            m_new = jnp.maximum(m, jnp.max(t, axis=0, keepdims=True))
            p_ref[slot, hh] = jnp.exp2(t - m_new).astype(BF16)
            new_ms.append(m_new)
            alphas.append(jnp.exp2(m - m_new))
        return tuple(new_ms), tuple(alphas)

    def values(kj, slot, alphas):
        for hh in range(heads):
            acc_ref[hh] = alphas[hh] * acc_ref[hh] + jnp.dot(
                vt_ref[hh, kj], p_ref[slot, hh], preferred_element_type=F32)

    nq = seq // bq

    def load_q(qi):
        return [jnp.concatenate([qn_ref[qrows(qi), hh * NOPE_DIM:(hh + 1) * NOPE_DIM],
                                 qr_ref[qrows(qi), hh * LANES:(hh + 1) * LANES]], axis=-1)
                for hh in range(heads)]

    scores(load_q(0), 0, 0)

    def q_body(qi, _):
        q = load_q(qi)
        acc_ref[...] = jnp.zeros(acc_ref.shape, F32)
        p_ref[1] = jnp.zeros(p_ref.shape[1:], BF16)
        ms = tuple(jnp.full((1, bq), -jnp.inf, F32) for _ in range(heads))
        alphas = tuple(jnp.ones((1, bq), F32) for _ in range(heads))

        def pair(b, carry):
            ms, alphas = carry
            values(jnp.maximum(b - 1, 0), 1, alphas)
            ms, alphas = softmax(0, ms)
            scores(q, b + 1, 1)
            values(b, 0, alphas)
            ms, alphas = softmax(1, ms)
            scores(q, b + 2, 0)
            return ms, alphas

        carry = lax.fori_loop(0, qi // 2, lambda u, c: pair(4 * u + 2, pair(4 * u, c)), (ms, alphas))
        ms, alphas = lax.fori_loop(qi - qi % 2, qi, lambda u, c: pair(2 * u, c), carry)
        b = 2 * qi
        values(jnp.maximum(b - 1, 0), 1, alphas)
        ms, alphas = softmax(0, ms, diag_offset=0)
        scores(q, b + 1, 1)
        scores(load_q(jnp.minimum(qi + 1, nq - 1)), 0, 0)
        values(b, 0, alphas)
        ms, alphas = softmax(1, ms, diag_offset=bk)
        values(b + 1, 1, alphas)
        for hh in range(heads):
            out = acc_ref[hh, :V_DIM, :] / acc_ref[hh, V_DIM:V_DIM + 1, :]
            o_ref[qrows(qi), hh * V_DIM:(hh + 1) * V_DIM] = out.T.astype(BF16)
        return 0

    lax.fori_loop(0, nq, q_body, 0)


def _attention(qn, qr, kn, kr, v, *, batch, seq):
    bk, heads = ATTN_BK, ATTN_HEADS
    bq = 2 * bk
    hn = N_HEADS * NOPE_DIM
    w = heads * LANES
    r3 = lambda a: a.reshape(batch, seq, a.shape[-1])
    hspec = pl.BlockSpec((None, seq, w), lambda b, hp: (b, 0, hp))
    krspec = pl.BlockSpec((None, seq, LANES), lambda b, hp: (b, 0, 0))
    kern = functools.partial(_attn_kernel, seq=seq, bk=bk, heads=heads)
    vext = V_DIM + SUBLANES
    scratch = heads * seq * vext * 2 + 2 * heads * bk * bq * (4 + 2) + heads * vext * bq * 4
    vmem = 2 * (5 * seq * w + seq * LANES) * 2 + scratch + heads * 8 * bk * bq * 4
    out = pl.pallas_call(
        kern,
        grid=(batch, N_HEADS // heads),
        in_specs=[hspec, hspec, hspec, krspec, hspec],
        out_specs=hspec,
        out_shape=jax.ShapeDtypeStruct((batch, seq, hn), BF16),
        scratch_shapes=[pltpu.VMEM((heads, seq // bk, vext, bk), BF16),
                        pltpu.VMEM((2, heads, bk, bq), F32),
                        pltpu.VMEM((2, heads, bk, bq), BF16),
                        pltpu.VMEM((heads, vext, bq), F32)],
        compiler_params=_params(2, vmem),
        name="attention",
    )(r3(qn), r3(qr), r3(kn), r3(kr), r3(v))
    return out.reshape(batch * seq, hn)


def _out_proj_kernel(yp_ref, ya_ref, w_ref, h_ref, g_ref, csrc_ref, o_ref, xn_ref, cdst_ref, *, bm):
    cdst_ref[...] = csrc_ref[...].astype(BF16)
    for r0 in range(0, bm, ROW_SUB):
        rs = slice(r0, r0 + ROW_SUB)
        y = jnp.concatenate([yp_ref[rs, :], ya_ref[rs, :]], axis=-1)
        hn = h_ref[rs, :] + jnp.dot(y, w_ref[...], preferred_element_type=F32)
        o_ref[rs, :] = hn
        xn_ref[rs, :] = _rms(hn, g_ref[...]).astype(BF16)


def _out_proj(yp, ya, w, h, layer, g, w_cast):
    t, d = h.shape
    bm = OUT_BM
    row = lambda n: pl.BlockSpec((bm, n), lambda i: (i, 0))
    c_in, c_out, c_shape, c_bytes = _cast_stream(w_cast, layer, t // bm, lambda i: i)
    vmem = w.size * 2 + 2 * bm * (yp.shape[1] + ya.shape[1]) * 2 + 4 * bm * d * 4 + 2 * bm * d * 2
    return pl.pallas_call(
        functools.partial(_out_proj_kernel, bm=bm),
        grid=(t // bm,),
        in_specs=[row(yp.shape[1]), row(ya.shape[1]), _resident(w.shape), row(d),
                  _layer_resident(g, layer), c_in],
        out_specs=[row(d), row(d), c_out],
        out_shape=[jax.ShapeDtypeStruct((t, d), F32), jax.ShapeDtypeStruct((t, d), BF16), c_shape],
        compiler_params=_params(1, vmem + 4 * bm * d * 4 + c_bytes + (4 << 20)),
        name="out_proj",
    )(yp, ya, w, h, g, w_cast)


def _shift_rows(cur, prev_tail, k):
    rolled = pltpu.roll(cur, k, 0)
    first = lax.broadcasted_iota(jnp.int32, prev_tail.shape, 0) < k
    head = jnp.where(first, pltpu.roll(prev_tail, k, 0), rolled[:SUBLANES, :])
    return jnp.concatenate([head, rolled[SUBLANES:, :]], axis=0)


def _ffn_up_kernel(xn_ref, wg_ref, wu_ref, cw_ref, cb_ref, csrc_ref, o_ref, cdst_ref,
                   wgb_ref, wub_ref, carry_ref, *, bm, tiles_per_seq):
    cdst_ref[...] = csrc_ref[...].astype(BF16)
    i = pl.program_id(1)

    @pl.when(i == 0)
    def _():
        wgb_ref[...] = wg_ref[...].astype(BF16)
        wub_ref[...] = wu_ref[...].astype(BF16)

    cw = cw_ref[...]
    cb = cb_ref[...]
    @pl.when(i % tiles_per_seq == 0)
    def _():
        carry_ref[...] = jnp.zeros(carry_ref.shape, F32)

    tail = carry_ref[...]
    for r0 in range(0, bm, FFN_UP_SUB):
        rs = slice(r0, r0 + FFN_UP_SUB)
        x = xn_ref[rs, :]
        gate = jnp.dot(x, wgb_ref[...], preferred_element_type=F32)
        up = jnp.dot(x, wub_ref[...], preferred_element_type=F32)
        g1 = _shift_rows(gate, tail, 1)
        g2 = _shift_rows(gate, tail, 2)
        c = cw[0:1, :] * g2 + cw[1:2, :] * g1 + cw[2:3, :] * gate + cb
        o_ref[rs, :] = (c * _sigmoid(c) * up).astype(BF16)
        tail = gate[FFN_UP_SUB - SUBLANES:, :]
    carry_ref[...] = tail


def _ffn_up(xn, w_up, layer, conv_w, conv_b, w_cast, *, seq):
    t, d = xn.shape
    d_ff = conv_w.shape[-1]
    bm, bn = FFN_UP_BM, FFN_UP_BN
    nj = d_ff // bn
    kern = functools.partial(_ffn_up_kernel, bm=bm, tiles_per_seq=seq // bm)
    ni = t // bm
    c_in, c_out, c_shape, c_bytes = _cast_stream(w_cast, layer, nj * ni, lambda j, i: j * ni + i)
    vmem = (2 * bm * d * 2 + 2 * 2 * d * bn * 4 + 2 * d * bn * 2 + 2 * bm * bn * 2
            + 10 * FFN_UP_SUB * bn * 4 + c_bytes)
    return pl.pallas_call(
        kern,
        grid=(nj, ni),
        in_specs=[pl.BlockSpec((bm, d), lambda j, i: (i, 0)),
                  pl.BlockSpec((None, d, bn), lambda j, i: (layer, 0, j)),
                  pl.BlockSpec((None, d, bn), lambda j, i: (layer, 0, j + nj)),
                  pl.BlockSpec((None, CONV_W, bn), lambda j, i: (layer, 0, j)),
                  pl.BlockSpec((None, 1, bn), lambda j, i: (layer, 0, j)), c_in],
        out_specs=[pl.BlockSpec((bm, bn), lambda j, i: (i, j)), c_out],
        out_shape=[jax.ShapeDtypeStruct((t, d_ff), BF16), c_shape],
        scratch_shapes=[pltpu.VMEM((d, bn), BF16), pltpu.VMEM((d, bn), BF16),
                        pltpu.VMEM((SUBLANES, bn), F32)],
        compiler_params=_params(2, vmem + (4 << 20)),
        name="ffn_up",
    )(xn, w_up, w_up, conv_w, conv_b, w_cast)


def _ffn_down_kernel(a_ref, w_ref, h_ref, o_ref):
    o_ref[...] = h_ref[...] + jnp.dot(a_ref[...], w_ref[...], preferred_element_type=F32)


def _ffn_down(act, w, h):
    t, d = h.shape
    d_ff = act.shape[1]
    bm = FFN_DOWN_BM
    vmem = w.size * 2 + 2 * bm * d_ff * 2 + 4 * bm * d * 4 + 2 * bm * d * 4
    return pl.pallas_call(
        _ffn_down_kernel,
        grid=(t // bm,),
        in_specs=[pl.BlockSpec((bm, d_ff), lambda i: (i, 0)), _resident(w.shape),
                  pl.BlockSpec((bm, d), lambda i: (i, 0))],
        out_specs=pl.BlockSpec((bm, d), lambda i: (i, 0)),
        out_shape=jax.ShapeDtypeStruct((t, d), F32),
        compiler_params=_params(1, vmem + (4 << 20)),
        name="ffn_down",
    )(act, w, h)


def _ple_kernel(h_ref, g_ref, wgate_ref, p_ref, wple_ref, gfin_ref, o_ref, *, bm, final):
    for r0 in range(0, bm, ROW_SUB):
        rs = slice(r0, r0 + ROW_SUB)
        h = h_ref[rs, :]
        xn = _rms(h, g_ref[...]).astype(BF16)
        gate = _sigmoid(jnp.dot(xn, wgate_ref[...], preferred_element_type=F32))
        emb = jnp.dot(p_ref[rs, :].astype(BF16), wple_ref[...], preferred_element_type=F32)
        out = h + emb * gate
        if final:
            out = _rms(out, gfin_ref[...])
        o_ref[rs, :] = out


def _ple(h, layer, g, w_gate, p, w_ple, g_final, *, final):
    t, d = h.shape
    bm = PLE_BM
    row = lambda n: pl.BlockSpec((bm, n), lambda i: (i, 0))
    ple_dim = p.shape[-1]
    vmem = (w_gate.size * 2 + _slab_bytes(w_ple) + 4 * bm * d * 4 + 2 * bm * ple_dim * 4
            + 5 * bm * d * 4)
    return pl.pallas_call(
        functools.partial(_ple_kernel, bm=bm, final=final),
        grid=(t // bm,),
        in_specs=[row(d), _layer_resident(g, layer), _resident(w_gate.shape),
                  pl.BlockSpec((None, bm, ple_dim), lambda i: (layer, i, 0)),
                  _layer_resident(w_ple, layer), _resident(g_final.shape)],
        out_specs=row(d),
        out_shape=jax.ShapeDtypeStruct((t, d), F32),
        compiler_params=_params(1, vmem + (4 << 20)),
        name="ple",
    )(h, g, w_gate, p, w_ple, g_final)


def _rope_cols(w):
    half = ROPE_DIM // 2
    x1, x2 = w[..., :half], w[..., half:]
    pad = jnp.zeros(w.shape[:-1] + (LANES - ROPE_DIM,), w.dtype)
    return jnp.concatenate([x1, x2, pad], axis=-1), jnp.concatenate([-x2, x1, pad], axis=-1)


def _w_in_ext_kernel(w_ref, o_ref, *, o3):
    half = ROPE_DIM // 2
    o_ref[:, :o3] = w_ref[:, :o3].astype(BF16)
    x = w_ref[:, o3:].astype(BF16)
    x1, x2 = x[:, :half], x[:, half:]
    pad = jnp.zeros((x.shape[0], LANES - ROPE_DIM), BF16)
    r = jnp.concatenate([x1, x2, pad], axis=1)
    rot = jnp.concatenate([-x2, x1, pad], axis=1)
    o_ref[:, o3:o3 + LANES] = r
    o_ref[:, o3 + LANES:] = rot


def _prep_w_in(w_in, o3):
    depth, d, n = w_in.shape
    chunk = 256
    return pl.pallas_call(
        functools.partial(_w_in_ext_kernel, o3=o3),
        grid=(depth, d // chunk),
        in_specs=[pl.BlockSpec((None, chunk, n), lambda l, i: (l, i, 0))],
        out_specs=pl.BlockSpec((None, chunk, o3 + 2 * LANES), lambda l, i: (l, i, 0)),
        out_shape=jax.ShapeDtypeStruct((depth, d, o3 + 2 * LANES), BF16),
        compiler_params=_params(2, 8 * chunk * (o3 + 2 * LANES) * 4),
        name="w_in_ext",
    )(w_in)


def _prep_w_uq(w_uq):
    w = w_uq.astype(BF16)
    hd = NOPE_DIM + ROPE_DIM
    head = lambda h, a, b: w[..., h * hd + a:h * hd + b]
    nope = [head(h, 0, NOPE_DIM) for h in range(N_HEADS)]
    ropes = [_rope_cols(head(h, NOPE_DIM, hd)) for h in range(N_HEADS)]
    return jnp.concatenate(nope + [r[0] for r in ropes] + [r[1] for r in ropes], axis=-1)


def _prep_w_ukv(w_ukv):
    w = w_ukv.astype(BF16)
    hd = NOPE_DIM + V_DIM
    k_nope = [w[..., h * hd:h * hd + NOPE_DIM] for h in range(N_HEADS)]
    v = [w[..., h * hd + NOPE_DIM:(h + 1) * hd] for h in range(N_HEADS)]
    return jnp.concatenate(k_nope + v, axis=-1)


def kernel(x, p, positions, norm_mix_g, w_in, pool_w, pool_scale, q_norm_g, w_uq, kv_norm_g, w_ukv,
           w_out, norm_ffn_g, w_up, conv_w, conv_b, w_down, norm_ple_g, w_ple, w_ple_gate, final_norm_g):
    batch, seq, d = x.shape
    depth = w_in.shape[0]
    t = batch * seq
    d_pool = pool_scale.shape[1]
    q_lora = q_norm_g.shape[1]
    kv_lora = kv_norm_g.shape[1]
    o3 = d_pool + q_lora + kv_lora

    inv_freq = 1.0 / (ROPE_THETA ** (jnp.arange(0, ROPE_DIM, 2, dtype=F32) / ROPE_DIM))
    ang = positions.astype(F32)[..., None] * inv_freq
    zpad = jnp.zeros((batch, seq, LANES - ROPE_DIM), F32)
    cos128 = jnp.concatenate([jnp.cos(ang), jnp.cos(ang), zpad], axis=-1).reshape(t, LANES)
    sin128 = jnp.concatenate([jnp.sin(ang), jnp.sin(ang), zpad], axis=-1).reshape(t, LANES)

    rows = lambda a: a.reshape(depth, 1, -1)
    win_b, wq_b, wkv_b = _prep_w_in(w_in, o3), _prep_w_uq(w_uq), _prep_w_ukv(w_ukv)
    pw_b, wple_b = pool_w.astype(BF16), w_ple.astype(BF16)
    g_mix, g_q, g_kv, g_ffn, g_ple = (rows(a) for a in (norm_mix_g, q_norm_g, kv_norm_g,
                                                        norm_ffn_g, norm_ple_g))
    ps, cb = rows(pool_scale), rows(conv_b)
    p3 = p.reshape(depth, t, p.shape[-1])

    h = x.reshape(t, d)
    for i in range(depth):
        ypool, qn, qr, kn, v, kr, wout_b = _front(
            h, i, g_mix, win_b, cos128, sin128, pw_b, ps, g_q, wq_b, g_kv, wkv_b, w_out,
            seq=seq, d_pool=d_pool, q_lora=q_lora, kv_lora=kv_lora)
        yatt = _attention(qn, qr, kn, kr, v, batch=batch, seq=seq)
        h, xn, wgate_b = _out_proj(ypool, yatt, wout_b, h, i, g_ffn, w_ple_gate)
        act, wdown_b = _ffn_up(xn, w_up, i, conv_w, cb, w_down, seq=seq)
        h = _ffn_down(act, wdown_b, h)
        h = _ple(h, i, g_ple, wgate_b, p3, wple_b, final_norm_g.reshape(1, -1),
                 final=(i == depth - 1))
    return h.reshape(batch, seq, d)
```

```python
import functools
import math

import jax
import jax.numpy as jnp
from jax import lax
from jax.experimental import pallas as pl
from jax.experimental.pallas import tpu as pltpu

F32 = jnp.float32
BF16 = jnp.bfloat16

POOL_WINDOWS = (2, 4, 8, 16)
N_POOL_GROUPS = len(POOL_WINDOWS)
N_HEADS = 8
NOPE_DIM = 128
ROPE_DIM = 64
V_DIM = 128
ROPE_THETA = 10000.0
ATTN_C2 = math.log2(math.e) / math.sqrt(NOPE_DIM + ROPE_DIM)
CONV_W = 3
EPS = 1e-6

LANES = 128
SUBLANES = 8
V7X_SCOPED_VMEM_BYTES = 60000 * 1024

POOL_HALO = 16

FRONT_BM = 512
ATTN_BK = 256
ATTN_HEADS = 2
OUT_BM = 512
FFN_UP_BM = 1024
FFN_UP_SUB = 256
FFN_UP_BN = 512
FFN_DOWN_BM = 256
ROW_SUB = 256


def _rms(x, g):
    ms = jnp.mean(x * x, axis=-1, keepdims=True)
    return (x * lax.rsqrt(ms + EPS)) * g


def _sigmoid(x):
    return 1.0 / (1.0 + jnp.exp(-x))


def _resident(shape):
    nd = len(shape)
    return pl.BlockSpec(shape, lambda *_: (0,) * nd, pipeline_mode=pl.Buffered(1))


def _layer_resident(stacked, layer):
    shape = stacked.shape[1:]
    return pl.BlockSpec((None,) + shape, lambda *_: (layer,) + (0,) * len(shape),
                        pipeline_mode=pl.Buffered(1))


def _cast_stream(stacked, layer, n_steps, step_of):
    rows, cols = stacked.shape[1:]
    chunk = rows // n_steps
    assert chunk * n_steps == rows and chunk % (2 * SUBLANES) == 0, (rows, n_steps)
    in_spec = pl.BlockSpec((None, chunk, cols), lambda *idx: (layer, step_of(*idx), 0))
    out_spec = pl.BlockSpec((chunk, cols), lambda *idx: (step_of(*idx), 0))
    return in_spec, out_spec, jax.ShapeDtypeStruct((rows, cols), BF16), chunk * cols * (4 + 2) * 2


def _slab_bytes(stacked):
    return math.prod(stacked.shape[1:]) * stacked.dtype.itemsize


def _params(n_axes, vmem_bytes):
    return pltpu.CompilerParams(
        dimension_semantics=("arbitrary",) * n_axes,
        vmem_limit_bytes=min(int(vmem_bytes), V7X_SCOPED_VMEM_BYTES),
    )


def _front_kernel(h_ref, g_ref, win_ref, cos_ref, sin_ref, pw_ref, ps_ref,
                  gq_ref, wq_ref, gkv_ref, wkv_ref, csrc_ref,
                  ypool_ref, qn_ref, qr_ref, kn_ref, v_ref, kr_ref, cdst_ref,
                  carry_ref, *, bm, tiles_per_seq, d_pool, q_lora, kv_lora):
    cdst_ref[...] = csrc_ref[...].astype(BF16)
    i = pl.program_id(0)
    seq_tile = i % tiles_per_seq
    pool_c = d_pool // N_POOL_GROUPS
    o1 = d_pool
    o2 = o1 + q_lora
    o3 = o2 + kv_lora
    hn = N_HEADS * NOPE_DIM
    hr = N_HEADS * LANES

    @pl.when(seq_tile == 0)
    def _():
        carry_ref[...] = jnp.zeros(carry_ref.shape, F32)

    tail = carry_ref[...]
    for r0 in range(0, bm, ROW_SUB):
        rs = slice(r0, r0 + ROW_SUB)
        xn = _rms(h_ref[rs, :], g_ref[...]).astype(BF16)
        u = jnp.dot(xn, win_ref[...], preferred_element_type=F32)

        cos = cos_ref[rs, :]
        sin = sin_ref[rs, :]
        kr_ref[rs, :] = (u[:, o3:o3 + LANES] * cos + u[:, o3 + LANES:o3 + 2 * LANES] * sin).astype(BF16)

        cqn = _rms(u[:, o1:o2], gq_ref[...]).astype(BF16)
        q = jnp.dot(cqn, wq_ref[...], preferred_element_type=F32)
        qn_ref[rs, :] = (q[:, :hn] * ATTN_C2).astype(BF16)
        cos_h = jnp.concatenate([cos] * N_HEADS, axis=1)
        sin_h = jnp.concatenate([sin] * N_HEADS, axis=1)
        q_rope = q[:, hn:hn + hr] * cos_h + q[:, hn + hr:hn + 2 * hr] * sin_h
        qr_ref[rs, :] = (q_rope * ATTN_C2).astype(BF16)

        ckvn = _rms(u[:, o2:o3], gkv_ref[...]).astype(BF16)
        kv = jnp.dot(ckvn, wkv_ref[...], preferred_element_type=F32)
        kn_ref[rs, :] = kv[:, :hn].astype(BF16)
        v_ref[rs, :] = kv[:, hn:].astype(BF16)

        up = u[:, :o1]
        t1 = lax.broadcasted_iota(jnp.int32, (ROW_SUB, LANES), 0) + (seq_tile * bm + r0 + 1)
        for grp, w in enumerate(POOL_WINDOWS):
            c0, c1 = grp * pool_c, (grp + 1) * pool_c
            ext = jnp.concatenate([tail[:, c0:c1], up[:, c0:c1]], axis=0)
            acc = ext + pltpu.roll(ext, 1, 0)
            span = 2
            while span < w:
                acc = acc + pltpu.roll(acc, span, 0)
                span *= 2
            inv = 1.0 / jnp.minimum(t1, w).astype(F32)
            inv = jnp.concatenate([inv] * (pool_c // LANES), axis=1)
            diff = (acc[POOL_HALO:, :] * inv - up[:, c0:c1]).astype(BF16)
            y = jnp.dot(diff, pw_ref[grp], preferred_element_type=F32) * ps_ref[:, c0:c1]
            ypool_ref[rs, c0:c1] = y.astype(BF16)
        tail = up[ROW_SUB - POOL_HALO:, :]
    carry_ref[...] = tail


def _front(h, layer, g, win, cos128, sin128, pw, ps, gq, wq, gkv, wkv, w_cast, *,
           seq, d_pool, q_lora, kv_lora):
    t, d = h.shape
    bm = FRONT_BM
    hn = N_HEADS * NOPE_DIM
    row = lambda n: pl.BlockSpec((bm, n), lambda i: (i, 0))
    res = lambda a: _layer_resident(a, layer)
    kern = functools.partial(_front_kernel, bm=bm, tiles_per_seq=seq // bm,
                             d_pool=d_pool, q_lora=q_lora, kv_lora=kv_lora)
    weights = _slab_bytes(win) + _slab_bytes(wq) + _slab_bytes(wkv) + _slab_bytes(pw)
    tiles = 2 * bm * (d * 4 + 2 * LANES * 4 + (d_pool + 4 * hn + LANES) * 2)
    temps = bm * (win.shape[-1] + wq.shape[-1] + wkv.shape[-1] + 4 * d_pool) * 4 + bm * d * 6
    c_in, c_out, c_shape, c_bytes = _cast_stream(w_cast, layer, t // bm, lambda i: i)
    return pl.pallas_call(
        kern,
        grid=(t // bm,),
        in_specs=[row(d), res(g), res(win), row(LANES), row(LANES),
                  res(pw), res(ps), res(gq), res(wq), res(gkv), res(wkv), c_in],
        out_specs=[row(d_pool), row(hn), row(hn), row(hn), row(hn), row(LANES), c_out],
        out_shape=[jax.ShapeDtypeStruct((t, d_pool), BF16)] + [jax.ShapeDtypeStruct((t, hn), BF16)] * 4
                  + [jax.ShapeDtypeStruct((t, LANES), BF16), c_shape],
        scratch_shapes=[pltpu.VMEM((POOL_HALO, d_pool), F32)],
        compiler_params=_params(1, weights + tiles + temps + c_bytes),
        name="front",
    )(h, g, win, cos128, sin128, pw, ps, gq, wq, gkv, wkv, w_cast)


def _attn_kernel(qn_ref, qr_ref, kn_ref, kr_ref, v_ref, o_ref,
                 vt_ref, t_ref, p_ref, acc_ref, *, seq, bk, heads):
    bq = 2 * bk
    nt_dims = (((1,), (1,)), ((), ()))

    def krows(c):
        return pl.ds(pl.multiple_of(c * bk, bk), bk)

    def qrows(c):
        return pl.ds(pl.multiple_of(c * bq, bq), bq)

    ones_rows = (lax.broadcasted_iota(jnp.int32, (SUBLANES, bk), 0) == 0).astype(F32)

    def load_vt(c, _):
        for hh in range(heads):
            v = v_ref[krows(c), hh * V_DIM:(hh + 1) * V_DIM]
            vt_ref[hh, c] = jnp.concatenate([v.astype(F32).T, ones_rows], axis=0).astype(BF16)
        return 0

    lax.fori_loop(0, seq // bk, load_vt, 0)

    def scores(q, kj, slot):
        kr = kr_ref[krows(kj), :]
        for hh in range(heads):
            k = jnp.concatenate([kn_ref[krows(kj), hh * NOPE_DIM:(hh + 1) * NOPE_DIM], kr], axis=-1)
            t_ref[slot, hh] = lax.dot_general(k, q[hh], nt_dims, preferred_element_type=F32)

    def softmax(slot, ms, diag_offset=None):
        new_ms, alphas = [], []
        for hh in range(heads):
            m = ms[hh]
            t = t_ref[slot, hh]
            if diag_offset is not None:
                key = lax.broadcasted_iota(jnp.int32, (bk, bq), 0) + diag_offset
                qry = lax.broadcasted_iota(jnp.int32, (bk, bq), 1)
                t = jnp.where(key <= qry, t, -jnp.inf)
            m_new = jnp.maximum(m, jnp.max(t, axis=0, keepdims=True))
            p_ref[slot, hh] = jnp.exp2(t - m_new).astype(BF16)
            new_ms.append(m_new)
            alphas.append(jnp.exp2(m - m_new))
        return tuple(new_ms), tuple(alphas)

    def values(kj, slot, alphas):
        for hh in range(heads):
            acc_ref[hh] = alphas[hh] * acc_ref[hh] + jnp.dot(
                vt_ref[hh, kj], p_ref[slot, hh], preferred_element_type=F32)

    nq = seq // bq

    def load_q(qi):
        return [jnp.concatenate([qn_ref[qrows(qi), hh * NOPE_DIM:(hh + 1) * NOPE_DIM],
                                 qr_ref[qrows(qi), hh * LANES:(hh + 1) * LANES]], axis=-1)
                for hh in range(heads)]

    scores(load_q(0), 0, 0)

    def q_body(qi, _):
        q = load_q(qi)
        acc_ref[...] = jnp.zeros(acc_ref.shape, F32)
        p_ref[1] = jnp.zeros(p_ref.shape[1:], BF16)
        ms = tuple(jnp.full((1, bq), -jnp.inf, F32) for _ in range(heads))
        alphas = tuple(jnp.ones((1, bq), F32) for _ in range(heads))

        def pair(b, carry):
            ms, alphas = carry
            values(jnp.maximum(b - 1, 0), 1, alphas)
            ms, alphas = softmax(0, ms)
            scores(q, b + 1, 1)
            values(b, 0, alphas)
            ms, alphas = softmax(1, ms)
            scores(q, b + 2, 0)
            return ms, alphas

        carry = lax.fori_loop(0, qi // 2, lambda u, c: pair(4 * u + 2, pair(4 * u, c)), (ms, alphas))
        ms, alphas = lax.fori_loop(qi - qi % 2, qi, lambda u, c: pair(2 * u, c), carry)
        b = 2 * qi
        values(jnp.maximum(b - 1, 0), 1, alphas)
        ms, alphas = softmax(0, ms, diag_offset=0)
        scores(q, b + 1, 1)
        scores(load_q(jnp.minimum(qi + 1, nq - 1)), 0, 0)
        values(b, 0, alphas)
        ms, alphas = softmax(1, ms, diag_offset=bk)
        values(b + 1, 1, alphas)
        for hh in range(heads):
            out = acc_ref[hh, :V_DIM, :] / acc_ref[hh, V_DIM:V_DIM + 1, :]
            o_ref[qrows(qi), hh * V_DIM:(hh + 1) * V_DIM] = out.T.astype(BF16)
        return 0

    lax.fori_loop(0, nq, q_body, 0)


def _attention(qn, qr, kn, kr, v, *, batch, seq):
    bk, heads = ATTN_BK, ATTN_HEADS
    bq = 2 * bk
    hn = N_HEADS * NOPE_DIM
    w = heads * LANES
    r3 = lambda a: a.reshape(batch, seq, a.shape[-1])
    hspec = pl.BlockSpec((None, seq, w), lambda b, hp: (b, 0, hp))
    krspec = pl.BlockSpec((None, seq, LANES), lambda b, hp: (b, 0, 0))
    kern = functools.partial(_attn_kernel, seq=seq, bk=bk, heads=heads)
    vext = V_DIM + SUBLANES
    scratch = heads * seq * vext * 2 + 2 * heads * bk * bq * (4 + 2) + heads * vext * bq * 4
    vmem = 2 * (5 * seq * w + seq * LANES) * 2 + scratch + heads * 8 * bk * bq * 4
    out = pl.pallas_call(
        kern,
        grid=(batch, N_HEADS // heads),
        in_specs=[hspec, hspec, hspec, krspec, hspec],
        out_specs=hspec,
        out_shape=jax.ShapeDtypeStruct((batch, seq, hn), BF16),
        scratch_shapes=[pltpu.VMEM((heads, seq // bk, vext, bk), BF16),
                        pltpu.VMEM((2, heads, bk, bq), F32),
                        pltpu.VMEM((2, heads, bk, bq), BF16),
                        pltpu.VMEM((heads, vext, bq), F32)],
        compiler_params=_params(2, vmem),
        name="attention",
    )(r3(qn), r3(qr), r3(kn), r3(kr), r3(v))
    return out.reshape(batch * seq, hn)


def _out_proj_kernel(yp_ref, ya_ref, w_ref, h_ref, g_ref, csrc_ref, o_ref, xn_ref, cdst_ref, *, bm):
    cdst_ref[...] = csrc_ref[...].astype(BF16)
    for r0 in range(0, bm, ROW_SUB):
        rs = slice(r0, r0 + ROW_SUB)
        y = jnp.concatenate([yp_ref[rs, :], ya_ref[rs, :]], axis=-1)
        hn = h_ref[rs, :] + jnp.dot(y, w_ref[...], preferred_element_type=F32)
        o_ref[rs, :] = hn
        xn_ref[rs, :] = _rms(hn, g_ref[...]).astype(BF16)


def _out_proj(yp, ya, w, h, layer, g, w_cast):
    t, d = h.shape
    bm = OUT_BM
    row = lambda n: pl.BlockSpec((bm, n), lambda i: (i, 0))
    c_in, c_out, c_shape, c_bytes = _cast_stream(w_cast, layer, t // bm, lambda i: i)
    vmem = w.size * 2 + 2 * bm * (yp.shape[1] + ya.shape[1]) * 2 + 4 * bm * d * 4 + 2 * bm * d * 2
    return pl.pallas_call(
        functools.partial(_out_proj_kernel, bm=bm),
        grid=(t // bm,),
        in_specs=[row(yp.shape[1]), row(ya.shape[1]), _resident(w.shape), row(d),
                  _layer_resident(g, layer), c_in],
        out_specs=[row(d), row(d), c_out],
        out_shape=[jax.ShapeDtypeStruct((t, d), F32), jax.ShapeDtypeStruct((t, d), BF16), c_shape],
        compiler_params=_params(1, vmem + 4 * bm * d * 4 + c_bytes + (4 << 20)),
        name="out_proj",
    )(yp, ya, w, h, g, w_cast)


def _shift_rows(cur, prev_tail, k):
    rolled = pltpu.roll(cur, k, 0)
    first = lax.broadcasted_iota(jnp.int32, prev_tail.shape, 0) < k
    head = jnp.where(first, pltpu.roll(prev_tail, k, 0), rolled[:SUBLANES, :])
    return jnp.concatenate([head, rolled[SUBLANES:, :]], axis=0)


def _ffn_up_kernel(xn_ref, wg_ref, wu_ref, cw_ref, cb_ref, csrc_ref, o_ref, cdst_ref,
                   wgb_ref, wub_ref, carry_ref, *, bm, tiles_per_seq):
    cdst_ref[...] = csrc_ref[...].astype(BF16)
    i = pl.program_id(1)

    @pl.when(i == 0)
    def _():
        wgb_ref[...] = wg_ref[...].astype(BF16)
        wub_ref[...] = wu_ref[...].astype(BF16)

    cw = cw_ref[...]
    cb = cb_ref[...]
    @pl.when(i % tiles_per_seq == 0)
    def _():
        carry_ref[...] = jnp.zeros(carry_ref.shape, F32)

    tail = carry_ref[...]
    for r0 in range(0, bm, FFN_UP_SUB):
        rs = slice(r0, r0 + FFN_UP_SUB)
        x = xn_ref[rs, :]
        gate = jnp.dot(x, wgb_ref[...], preferred_element_type=F32)
        up = jnp.dot(x, wub_ref[...], preferred_element_type=F32)
        g1 = _shift_rows(gate, tail, 1)
        g2 = _shift_rows(gate, tail, 2)
        c = cw[0:1, :] * g2 + cw[1:2, :] * g1 + cw[2:3, :] * gate + cb
        o_ref[rs, :] = (c * _sigmoid(c) * up).astype(BF16)
        tail = gate[FFN_UP_SUB - SUBLANES:, :]
    carry_ref[...] = tail


def _ffn_up(xn, w_up, layer, conv_w, conv_b, w_cast, *, seq):
    t, d = xn.shape
    d_ff = conv_w.shape[-1]
    bm, bn = FFN_UP_BM, FFN_UP_BN
    nj = d_ff // bn
    kern = functools.partial(_ffn_up_kernel, bm=bm, tiles_per_seq=seq // bm)
    ni = t // bm
    c_in, c_out, c_shape, c_bytes = _cast_stream(w_cast, layer, nj * ni, lambda j, i: j * ni + i)
    vmem = (2 * bm * d * 2 + 2 * 2 * d * bn * 4 + 2 * d * bn * 2 + 2 * bm * bn * 2
            + 10 * FFN_UP_SUB * bn * 4 + c_bytes)
    return pl.pallas_call(
        kern,
        grid=(nj, ni),
        in_specs=[pl.BlockSpec((bm, d), lambda j, i: (i, 0)),
                  pl.BlockSpec((None, d, bn), lambda j, i: (layer, 0, j)),
                  pl.BlockSpec((None, d, bn), lambda j, i: (layer, 0, j + nj)),
                  pl.BlockSpec((None, CONV_W, bn), lambda j, i: (layer, 0, j)),
                  pl.BlockSpec((None, 1, bn), lambda j, i: (layer, 0, j)), c_in],
        out_specs=[pl.BlockSpec((bm, bn), lambda j, i: (i, j)), c_out],
        out_shape=[jax.ShapeDtypeStruct((t, d_ff), BF16), c_shape],
        scratch_shapes=[pltpu.VMEM((d, bn), BF16), pltpu.VMEM((d, bn), BF16),
                        pltpu.VMEM((SUBLANES, bn), F32)],
        compiler_params=_params(2, vmem + (4 << 20)),
        name="ffn_up",
    )(xn, w_up, w_up, conv_w, conv_b, w_cast)


def _ffn_down_ple_kernel(a_ref, wd_ref, h_ref, g_ref, wgate_ref, p_ref, wple_ref, gfin_ref, o_ref,
                         *, final):
    h2 = h_ref[...] + jnp.dot(a_ref[...], wd_ref[...], preferred_element_type=F32)
    xn = _rms(h2, g_ref[...]).astype(BF16)
    gate = _sigmoid(jnp.dot(xn, wgate_ref[...], preferred_element_type=F32))
    emb = jnp.dot(p_ref[...].astype(BF16), wple_ref[...], preferred_element_type=F32)
    out = h2 + emb * gate
    if final:
        out = _rms(out, gfin_ref[...])
    o_ref[...] = out


def _ffn_down_ple(act, w_down, h, layer, g, w_gate, p, w_ple, g_final, *, final):
    t, d = h.shape
    d_ff = act.shape[1]
    bm = FFN_DOWN_BM
    ple_dim = p.shape[-1]
    row = lambda n: pl.BlockSpec((bm, n), lambda i: (i, 0))
    vmem = ((w_down.size + w_gate.size) * 2 + _slab_bytes(w_ple) + 2 * bm * d_ff * 2
            + 4 * bm * d * 4 + 2 * bm * ple_dim * 4 + 6 * bm * d * 4)
    return pl.pallas_call(
        functools.partial(_ffn_down_ple_kernel, final=final),
        grid=(t // bm,),
        in_specs=[row(d_ff), _resident(w_down.shape), row(d), _layer_resident(g, layer),
                  _resident(w_gate.shape),
                  pl.BlockSpec((None, bm, ple_dim), lambda i: (layer, i, 0)),
                  _layer_resident(w_ple, layer), _resident(g_final.shape)],
        out_specs=row(d),
        out_shape=jax.ShapeDtypeStruct((t, d), F32),
        compiler_params=_params(1, vmem + (4 << 20)),
        name="ffn_down_ple",
    )(act, w_down, h, g, w_gate, p, w_ple, g_final)


def _rope_cols(w):
    half = ROPE_DIM // 2
    x1, x2 = w[..., :half], w[..., half:]
    pad = jnp.zeros(w.shape[:-1] + (LANES - ROPE_DIM,), w.dtype)
    return jnp.concatenate([x1, x2, pad], axis=-1), jnp.concatenate([-x2, x1, pad], axis=-1)


def _w_in_ext_kernel(w_ref, o_ref, *, o3):
    half = ROPE_DIM // 2
    o_ref[:, :o3] = w_ref[:, :o3].astype(BF16)
    x = w_ref[:, o3:].astype(BF16)
    x1, x2 = x[:, :half], x[:, half:]
    pad = jnp.zeros((x.shape[0], LANES - ROPE_DIM), BF16)
    r = jnp.concatenate([x1, x2, pad], axis=1)
    rot = jnp.concatenate([-x2, x1, pad], axis=1)
    o_ref[:, o3:o3 + LANES] = r
    o_ref[:, o3 + LANES:] = rot


def _prep_w_in(w_in, o3):
    depth, d, n = w_in.shape
    chunk = 256
    return pl.pallas_call(
        functools.partial(_w_in_ext_kernel, o3=o3),
        grid=(depth, d // chunk),
        in_specs=[pl.BlockSpec((None, chunk, n), lambda l, i: (l, i, 0))],
        out_specs=pl.BlockSpec((None, chunk, o3 + 2 * LANES), lambda l, i: (l, i, 0)),
        out_shape=jax.ShapeDtypeStruct((depth, d, o3 + 2 * LANES), BF16),
        compiler_params=_params(2, 8 * chunk * (o3 + 2 * LANES) * 4),
        name="w_in_ext",
    )(w_in)


def _prep_w_uq(w_uq):
    w = w_uq.astype(BF16)
    hd = NOPE_DIM + ROPE_DIM
    head = lambda h, a, b: w[..., h * hd + a:h * hd + b]
    nope = [head(h, 0, NOPE_DIM) for h in range(N_HEADS)]
    ropes = [_rope_cols(head(h, NOPE_DIM, hd)) for h in range(N_HEADS)]
    return jnp.concatenate(nope + [r[0] for r in ropes] + [r[1] for r in ropes], axis=-1)


def _prep_w_ukv(w_ukv):
    w = w_ukv.astype(BF16)
    hd = NOPE_DIM + V_DIM
    k_nope = [w[..., h * hd:h * hd + NOPE_DIM] for h in range(N_HEADS)]
    v = [w[..., h * hd + NOPE_DIM:(h + 1) * hd] for h in range(N_HEADS)]
    return jnp.concatenate(k_nope + v, axis=-1)


def kernel(x, p, positions, norm_mix_g, w_in, pool_w, pool_scale, q_norm_g, w_uq, kv_norm_g, w_ukv,
           w_out, norm_ffn_g, w_up, conv_w, conv_b, w_down, norm_ple_g, w_ple, w_ple_gate, final_norm_g):
    batch, seq, d = x.shape
    depth = w_in.shape[0]
    t = batch * seq
    d_pool = pool_scale.shape[1]
    q_lora = q_norm_g.shape[1]
    kv_lora = kv_norm_g.shape[1]
    o3 = d_pool + q_lora + kv_lora

    inv_freq = 1.0 / (ROPE_THETA ** (jnp.arange(0, ROPE_DIM, 2, dtype=F32) / ROPE_DIM))
    ang = positions.astype(F32)[..., None] * inv_freq
    zpad = jnp.zeros((batch, seq, LANES - ROPE_DIM), F32)
    cos128 = jnp.concatenate([jnp.cos(ang), jnp.cos(ang), zpad], axis=-1).reshape(t, LANES)
    sin128 = jnp.concatenate([jnp.sin(ang), jnp.sin(ang), zpad], axis=-1).reshape(t, LANES)

    rows = lambda a: a.reshape(depth, 1, -1)
    win_b, wq_b, wkv_b = _prep_w_in(w_in, o3), _prep_w_uq(w_uq), _prep_w_ukv(w_ukv)
    pw_b, wple_b = pool_w.astype(BF16), w_ple.astype(BF16)
    g_mix, g_q, g_kv, g_ffn, g_ple = (rows(a) for a in (norm_mix_g, q_norm_g, kv_norm_g,
                                                        norm_ffn_g, norm_ple_g))
    ps, cb = rows(pool_scale), rows(conv_b)
    p3 = p.reshape(depth, t, p.shape[-1])

    h = x.reshape(t, d)
    for i in range(depth):
        ypool, qn, qr, kn, v, kr, wout_b = _front(
            h, i, g_mix, win_b, cos128, sin128, pw_b, ps, g_q, wq_b, g_kv, wkv_b, w_out,
            seq=seq, d_pool=d_pool, q_lora=q_lora, kv_lora=kv_lora)
        yatt = _attention(qn, qr, kn, kr, v, batch=batch, seq=seq)
        h, xn, wgate_b = _out_proj(ypool, yatt, wout_b, h, i, g_ffn, w_ple_gate)
        act, wdown_b = _ffn_up(xn, w_up, i, conv_w, cb, w_down, seq=seq)
        h = _ffn_down_ple(act, wdown_b, h, i, g_ple, wgate_b, p3, wple_b,
                          final_norm_g.reshape(1, -1), final=(i == depth - 1))
    return h.reshape(batch, seq, d)
```
